```python
import math
import jax, jax.numpy as jnp
from jax import lax
import numpy as np

D_MODEL = 1024
BATCH = 16
SEQ = 4096
DEPTH = 2

RW_HEADS = 8
RW_HEAD_DIM = 64
RW_W = RW_HEADS * RW_HEAD_DIM
DECAY_LORA = 64
ICLR_LORA = 64
GATE_LORA = 128
LNX_EPS = 64e-5
RW_COLS = 3 * RW_W + DECAY_LORA + ICLR_LORA + GATE_LORA
POOL_GROUPS = 4
POOL_W = 512
POOL_GW = POOL_W // POOL_GROUPS
POOL_WINDOWS = (2, 4, 8, 16)
POOL_MAXW = 16
CONV_W = 512
CONV_K = 3
ATT_HQ = 8
ATT_HKV = 2
ATT_G = ATT_HQ // ATT_HKV
ATT_HD = 64
ATT_WINDOW = 128
ATT_BLOCK = 128
ROPE_THETA = 10000.0
ATT_COLS = (ATT_HQ + 2 * ATT_HKV) * ATT_HD
N_BRANCH = 4
GATE_COLS = N_BRANCH * D_MODEL
IN_COLS = RW_COLS + POOL_W + 3 * CONV_W + ATT_COLS + GATE_COLS
IN_SPLITS = (RW_COLS, RW_COLS + POOL_W, RW_COLS + POOL_W + 3 * CONV_W, RW_COLS + POOL_W + 3 * CONV_W + ATT_COLS)
D_FF = ((8 * D_MODEL // 3 + 255) // 256) * 256
NORM_EPS = 1e-6

kernel_name = 'hybrid_gated_parallel_block'


def rms_norm(x, g):
    xf = x.astype(jnp.float32)
    y = xf * lax.rsqrt(jnp.mean(xf * xf, axis=-1, keepdims=True) + NORM_EPS)
    return (y * g.astype(jnp.float32)).astype(x.dtype)


def token_shift(u):
    return jnp.pad(u, ((0, 0), (1, 0), (0, 0)))[:, :-1]


def wkv7_scan(r, decay, k, v, kk, a):
    b, s, h, n = r.shape

    def step(state, inp):
        r_t, w_t, k_t, v_t, kk_t, a_t = inp
        s_kk = jnp.einsum('bhij,bhj->bhi', state, kk_t)
        state = (state * w_t[:, :, None, :]
                 - s_kk[..., None] * (kk_t * a_t)[:, :, None, :]
                 + v_t[..., None] * k_t[:, :, None, :])
        y_t = jnp.einsum('bhij,bhj->bhi', state, r_t)
        return state, y_t

    xs = tuple(jnp.moveaxis(t, 1, 0) for t in (r, decay, k, v, kk, a))
    state0 = jnp.zeros((b, h, n, n), jnp.float32)
    _, ys = lax.scan(step, state0, xs)
    return jnp.moveaxis(ys, 0, 1)


def rwkv7_mixer(p, mu, w_decay_up, w0, a_up, a0, g_up, k_k, k_a, r_k, lnx_g, lnx_b):
    dt = p.dtype
    b, s, _ = p.shape
    p = p.astype(jnp.float32)
    p = p + (token_shift(p) - p) * mu.astype(jnp.float32)
    r, k, v, wd, ad, gd = jnp.split(
        p, [RW_W, 2 * RW_W, 3 * RW_W, 3 * RW_W + DECAY_LORA, 3 * RW_W + DECAY_LORA + ICLR_LORA], axis=-1)
    w = -jax.nn.softplus(-(w0 + jnp.tanh(wd) @ w_decay_up)) - 0.5
    decay = jnp.exp(-jnp.exp(w))
    a = jax.nn.sigmoid(a0 + ad @ a_up)
    g = jax.nn.sigmoid(gd) @ g_up
    heads = lambda t: t.reshape(b, s, RW_HEADS, RW_HEAD_DIM)
    kk = heads(k * k_k)
    kk = kk / jnp.maximum(jnp.sqrt(jnp.sum(kk * kk, axis=-1, keepdims=True)), 1e-12)
    k = k * (1.0 + (a - 1.0) * k_a)
    rh, kh, vh, ah = heads(r), heads(k), heads(v), heads(a)
    y = wkv7_scan(rh, heads(decay), kh, vh, kk, ah)
    mean = jnp.mean(y, axis=-1, keepdims=True)
    var = jnp.mean(jnp.square(y - mean), axis=-1, keepdims=True)
    y = ((y - mean) * lax.rsqrt(var + LNX_EPS)).reshape(b, s, RW_W) * lnx_g + lnx_b
    bonus = jnp.sum(rh * kh * r_k.astype(jnp.float32), axis=-1, keepdims=True) * vh
    y = y + bonus.reshape(b, s, RW_W)
    return (y * g).astype(dt)


def pool_mixer(u, pool_w, pool_scale):
    dt = u.dtype
    b, s, _ = u.shape
    uf = u.astype(jnp.float32)
    cs = jnp.cumsum(uf, axis=1)
    cs_pad = jnp.pad(cs, ((0, 0), (POOL_MAXW, 0), (0, 0)))
    t = jnp.arange(s)
    outs = []
    for gi, win in enumerate(POOL_WINDOWS):
        sl = slice(gi * POOL_GW, (gi + 1) * POOL_GW)
        lag = cs_pad[:, POOL_MAXW - win:POOL_MAXW - win + s, sl]
        count = jnp.minimum(t + 1, win).astype(jnp.float32)[None, :, None]
        outs.append((cs[..., sl] - lag) / count - uf[..., sl])
    z = jnp.stack(outs, axis=2)
    z = jnp.einsum('bsgc,gcd->bsgd', z, pool_w.astype(jnp.float32)).reshape(b, s, POOL_W)
    return (z * pool_scale.astype(jnp.float32)).astype(dt)


def causal_depthwise_conv(u, w):
    c = u.shape[-1]
    return lax.conv_general_dilated(
        u, w[:, None, :].astype(u.dtype), window_strides=(1,), padding=[(CONV_K - 1, 0)],
        dimension_numbers=('NWC', 'WIO', 'NWC'), feature_group_count=c)


def short_conv_mixer(p, conv_w):
    b_gate, c_gate, u = jnp.split(p, [CONV_W, 2 * CONV_W], axis=-1)
    return b_gate * causal_depthwise_conv(c_gate * u, conv_w)


def rope(x, positions):
    half = x.shape[-1] // 2
    inv = ROPE_THETA ** (-jnp.arange(half, dtype=jnp.float32) * 2.0 / x.shape[-1])
    ang = positions.astype(jnp.float32)[..., None] * inv
    cos, sin = jnp.cos(ang)[:, :, None, :], jnp.sin(ang)[:, :, None, :]
    x1, x2 = x[..., :half], x[..., half:]
    return jnp.concatenate([x1 * cos - x2 * sin, x2 * cos + x1 * sin], axis=-1)


def swa_sink_attention(p, positions, q_norm_g, k_norm_g, sinks):
    dt = p.dtype
    b, s, _ = p.shape
    q, k, v = jnp.split(p, [ATT_HQ * ATT_HD, (ATT_HQ + ATT_HKV) * ATT_HD], axis=-1)
    q = rope(rms_norm(q.reshape(b, s, ATT_HQ, ATT_HD), q_norm_g).astype(jnp.float32), positions)
    k = rope(rms_norm(k.reshape(b, s, ATT_HKV, ATT_HD), k_norm_g).astype(jnp.float32), positions)
    v = v.reshape(b, s, ATT_HKV, ATT_HD).astype(jnp.float32)
    nb = s // ATT_BLOCK
    qb = q.reshape(b, nb, ATT_BLOCK, ATT_HKV, ATT_G, ATT_HD)

    def with_prev(t):
        t = t.reshape(b, nb, ATT_BLOCK, ATT_HKV, ATT_HD)
        prev = jnp.pad(t, ((0, 0), (1, 0), (0, 0), (0, 0), (0, 0)))[:, :-1]
        return jnp.concatenate([prev, t], axis=2)

    kc, vc = with_prev(k), with_prev(v)
    scores = jnp.einsum('bnqkgd,bnjkd->bnkgqj', qb, kc) * (ATT_HD ** -0.5)
    qi = jnp.arange(ATT_BLOCK)[:, None] + ATT_BLOCK
    kj = jnp.arange(2 * ATT_BLOCK)[None, :]
    dist = qi - kj
    band = (dist >= 0) & (dist < ATT_WINDOW)
    has_prev = (jnp.arange(nb)[:, None, None] > 0) | (kj >= ATT_BLOCK)[None]
    mask = band[None] & has_prev
    scores = jnp.where(mask[None, :, None, None], scores, -jnp.inf)
    sink = sinks.astype(jnp.float32).reshape(ATT_HKV, ATT_G)[None, None, :, :, None, None]
    m = jnp.maximum(jnp.max(scores, axis=-1, keepdims=True), sink)
    e = jnp.exp(scores - m)
    probs = e / (jnp.sum(e, axis=-1, keepdims=True) + jnp.exp(sink - m))
    o = jnp.einsum('bnkgqj,bnjkd->bnqkgd', probs, vc).reshape(b, s, ATT_HQ * ATT_HD)
    return o.astype(dt)


def setup_inputs(seed: int = 0) -> dict:
    key = jax.random.key(seed)
    ks = jax.random.split(key, 32)
    f32 = jnp.float32
    nrm = lambda i, shape, scale: jax.random.normal(ks[i], shape, f32) * scale
    L = DEPTH
    x = jax.random.normal(ks[0], (BATCH, SEQ, D_MODEL), f32)
    offset = jax.random.randint(ks[1], (BATCH, 1), 0, 1024, dtype=jnp.int32)
    positions = offset + jnp.arange(SEQ, dtype=jnp.int32)[None, :]
    return {
        'x': x,
        'positions': positions,
        'norm1_g': 1.0 + nrm(2, (L, D_MODEL), 0.02),
        'w_in': nrm(3, (L, D_MODEL, IN_COLS), D_MODEL ** -0.5),
        'shift_mu': jax.random.uniform(ks[4], (L, RW_COLS), f32),
        'w_decay_up': nrm(5, (L, DECAY_LORA, RW_W), 0.1 * DECAY_LORA ** -0.5),
        'w0': jax.random.uniform(ks[6], (L, RW_W), f32, -5.0, -0.5),
        'a_up': nrm(7, (L, ICLR_LORA, RW_W), ICLR_LORA ** -0.5),
        'a0': nrm(8, (L, RW_W), 0.1),
        'g_up': nrm(9, (L, GATE_LORA, RW_W), GATE_LORA ** -0.5),
        'k_k': 0.85 + nrm(10, (L, RW_W), 0.05),
        'k_a': 1.0 + nrm(11, (L, RW_W), 0.05),
        'r_k': nrm(12, (L, RW_HEADS, RW_HEAD_DIM), 0.1),
        'lnx_g': 1.0 + nrm(13, (L, RW_W), 0.02),
        'lnx_b': nrm(14, (L, RW_W), 0.02),
        'w_rwkv_out': nrm(15, (L, RW_W, D_MODEL), RW_W ** -0.5),
        'pool_w': nrm(16, (L, POOL_GROUPS, POOL_GW, POOL_GW), POOL_GW ** -0.5),
        'pool_scale': 1.0 + nrm(17, (L, POOL_W), 0.1),
        'w_pool_out': nrm(18, (L, POOL_W, D_MODEL), POOL_W ** -0.5),
        'conv_w': nrm(19, (L, CONV_K, CONV_W), CONV_K ** -0.5),
        'w_conv_out': nrm(20, (L, CONV_W, D_MODEL), CONV_W ** -0.5),
        'q_norm_g': 1.0 + nrm(21, (L, ATT_HD), 0.02),
        'k_norm_g': 1.0 + nrm(22, (L, ATT_HD), 0.02),
        'sinks': nrm(23, (L, ATT_HQ), 1.0),
        'w_attn_out': nrm(24, (L, ATT_HQ * ATT_HD, D_MODEL), (ATT_HQ * ATT_HD) ** -0.5),
        'w_o': nrm(25, (L, D_MODEL, D_MODEL), D_MODEL ** -0.5),
        'norm2_g': 1.0 + nrm(26, (L, D_MODEL), 0.02),
        'w_ffn_gate': nrm(27, (L, D_MODEL, D_FF), D_MODEL ** -0.5),
        'w_ffn_up': nrm(28, (L, D_MODEL, D_FF), D_MODEL ** -0.5),
        'w_ffn_down': nrm(29, (L, D_FF, D_MODEL), D_FF ** -0.5),
    }


def reference(x, positions, norm1_g, w_in, shift_mu, w_decay_up, w0, a_up, a0, g_up, k_k, k_a, r_k,
              lnx_g, lnx_b, w_rwkv_out, pool_w, pool_scale, w_pool_out, conv_w, w_conv_out,
              q_norm_g, k_norm_g, sinks, w_attn_out, w_o, norm2_g, w_ffn_gate, w_ffn_up, w_ffn_down):
    b, s, d = x.shape
    for i in range(DEPTH):
        h = rms_norm(x, norm1_g[i])
        p = h @ w_in[i]
        p_rw, p_pool, p_conv, p_att, p_gate = jnp.split(p, IN_SPLITS, axis=-1)
        y_a = rwkv7_mixer(p_rw, shift_mu[i], w_decay_up[i], w0[i], a_up[i], a0[i], g_up[i],
                          k_k[i], k_a[i], r_k[i], lnx_g[i], lnx_b[i]) @ w_rwkv_out[i]
        y_b = pool_mixer(p_pool, pool_w[i], pool_scale[i]) @ w_pool_out[i]
        y_c = short_conv_mixer(p_conv, conv_w[i]) @ w_conv_out[i]
        y_d = swa_sink_attention(p_att, positions, q_norm_g[i], k_norm_g[i], sinks[i]) @ w_attn_out[i]
        gates = jax.nn.sigmoid(p_gate.astype(jnp.float32)).reshape(b, s, N_BRANCH, d).astype(x.dtype)
        mixed = (gates[:, :, 0] * y_a + gates[:, :, 1] * y_b
                 + gates[:, :, 2] * y_c + gates[:, :, 3] * y_d)
        x = x + mixed @ w_o[i]
        h = rms_norm(x, norm2_g[i])
        x = x + (jax.nn.silu(h @ w_ffn_gate[i]) * (h @ w_ffn_up[i])) @ w_ffn_down[i]
    return x
```

```python
import functools
import math

import jax
import jax.numpy as jnp
from jax import lax
from jax.experimental import pallas as pl
from jax.experimental.pallas import tpu as pltpu

F32 = jnp.float32
BF16 = jnp.bfloat16

D_MODEL = 1024
RW_HEADS = 8
HEAD_DIM = 64
RW_W = RW_HEADS * HEAD_DIM
DECAY_LORA = 64
ICLR_LORA = 64
GATE_LORA = 128
LNX_EPS = 64e-5
RW_COLS = 3 * RW_W + DECAY_LORA + ICLR_LORA + GATE_LORA
POOL_W = 512
POOL_GW = 128
POOL_WINDOWS = (2, 4, 8, 16)
POOL_MAXW = 16
CONV_W = 512
CONV_K = 3
ATT_HQ = 8
ATT_HKV = 2
ATT_G = ATT_HQ // ATT_HKV
ATT_HD = 64
ATT_BLOCK = 128
ATT_COLS = (ATT_HQ + 2 * ATT_HKV) * ATT_HD
ROPE_THETA = 10000.0
N_BRANCH = 4
D_FF = 2816
NORM_EPS = 1e-6

LANES = 128
SUBLANES = 8
WKV_CHUNK = 64
SEQ_TILE = 512
TOK_TILE = 512
FFN_CHUNK = 1408
VMEM_LIMIT = 48 * 1024 * 1024


def _bf(x):
    return x.astype(BF16)


def _dot(a, b):
    return jnp.dot(a, b, preferred_element_type=F32)


def _dot_nt(a, b):
    return lax.dot_general(a, b, (((1,), (1,)), ((), ())), preferred_element_type=F32)


def _split2(x):
    hi = _bf(x)
    lo = _bf(x - hi.astype(F32))
    return hi, lo


def _dot_x2(x, m):
    hi, lo = _split2(x)
    return _dot(hi, m) + _dot(lo, m)


def _rmsnorm(x, g):
    ms = jnp.mean(x * x, axis=-1, keepdims=True)
    return x * lax.rsqrt(ms + NORM_EPS) * g


def _iota(shape, dim):
    return lax.broadcasted_iota(jnp.int32, shape, dim)


def _rope_kernel(pos_ref, inv_ref, cos_ref, sin_ref):
    ang = pos_ref[0].astype(F32) * inv_ref[...]
    lane = _iota(ang.shape, 1)
    cos_ref[0] = jnp.cos(ang)
    s = jnp.sin(ang)
    sin_ref[0] = jnp.where(lane < LANES // 2, -s, s)


def _rope_tables(positions, ts):
    b, s = positions.shape
    half = ATT_HD // 2
    inv = ROPE_THETA ** (-jnp.arange(half, dtype=F32) * 2.0 / ATT_HD)
    inv = jnp.tile(inv, LANES // half)[None, :]
    pos3 = positions.reshape(b, s, 1)
    out = jax.ShapeDtypeStruct((b, s, LANES), F32)
    return pl.pallas_call(
        _rope_kernel,
        out_shape=(out, out),
        grid=(b, s // ts),
        in_specs=[pl.BlockSpec((1, ts, 1), lambda i, j: (i, j, 0)),
                  pl.BlockSpec((1, LANES), lambda i, j: (0, 0))],
        out_specs=(pl.BlockSpec((1, ts, LANES), lambda i, j: (i, j, 0)),
                   pl.BlockSpec((1, ts, LANES), lambda i, j: (i, j, 0))),
        compiler_params=pltpu.CompilerParams(dimension_semantics=("arbitrary", "arbitrary")),
        name="rope_tables",
    )(pos3, inv)


def _rwkv_kernel(x_ref, g1_ref, wrw_ref, mu_ref, wdp_ref, wap_ref, gup_ref, w0_ref, a0_ref,
                 kk_ref, ka_ref, rk_ref, lng_ref, lnb_ref, o_ref,
                 pbuf, r_s, k_s, v_s, kap_s, beta_s, lw_s, bonus_s, g_s, y_s, st_s):
    j = pl.program_id(1)
    tb = x_ref.shape[1]
    L = WKV_CHUNK

    @pl.when(j == 0)
    def _():
        st_s[...] = jnp.zeros(st_s.shape, F32)
        pbuf[0:SUBLANES, :] = jnp.zeros((SUBLANES, RW_COLS), F32)

    gi = _iota((LANES, LANES), 0) // HEAD_DIM
    gj = _iota((LANES, LANES), 1) // HEAD_DIM
    seg_ones = jnp.where(gi == gj, 1.0, 0.0).astype(BF16)

    def segsum(z):
        return jnp.concatenate(
            [_dot_x2(z[:, LANES * b:LANES * (b + 1)], seg_ones) for b in range(RW_W // LANES)], axis=1)

    x = x_ref[0]
    h = _bf(_rmsnorm(x, g1_ref[...]))
    p = _dot(h, wrw_ref[...])
    pbuf[SUBLANES:SUBLANES + tb, :] = p
    p_prev = pbuf[SUBLANES - 1:SUBLANES - 1 + tb, :]
    pbuf[SUBLANES - 1:SUBLANES, :] = p[tb - 1:tb, :]
    pm = p + (p_prev - p) * mu_ref[...]
    r = pm[:, 0:RW_W]
    k = pm[:, RW_W:2 * RW_W]
    v = pm[:, 2 * RW_W:3 * RW_W]
    lora_in = pm[:, 3 * RW_W:3 * RW_W + LANES]
    gd = pm[:, 3 * RW_W + LANES:RW_COLS]
    z = w0_ref[...] + _dot(_bf(jnp.tanh(lora_in)), wdp_ref[...])
    lw_s[...] = (-math.exp(-0.5)) * jax.nn.sigmoid(z)
    a = jax.nn.sigmoid(a0_ref[...] + _dot(_bf(lora_in), wap_ref[...]))
    g_s[...] = _dot(_bf(jax.nn.sigmoid(gd)), gup_ref[...])
    kk = k * kk_ref[...]
    kap = kk / jnp.maximum(jnp.sqrt(segsum(kk * kk)), 1e-12)
    k2 = k * (1.0 + (a - 1.0) * ka_ref[...])
    r_s[...] = r
    k_s[...] = k2
    v_s[...] = v
    kap_s[...] = kap
    beta_s[...] = kap * a
    bonus_s[...] = segsum(r * k2 * rk_ref[...]) * v

    lane_lo = _iota((L, LANES), 1) < HEAD_DIM
    r2 = _iota((2 * L, 2 * L), 0)
    c2 = _iota((2 * L, 2 * L), 1)
    strict = ((r2 // L) == (c2 // L)) & ((c2 % L) < (r2 % L))
    incl = (_iota((L, 2 * L), 1) % L) <= _iota((L, 2 * L), 0)
    ltri = jnp.where(_iota((L, L), 1) <= _iota((L, L), 0), 1.0, 0.0).astype(BF16)

    def stack(zz):
        return jnp.concatenate([jnp.where(lane_lo, zz, 0.0), jnp.where(lane_lo, 0.0, zz)], axis=0)

    def chunk_body(ci, carry):
        t0 = pl.multiple_of(ci * L, L)
        rows = pl.ds(t0, L)
        lwc = lw_s[rows, :]
        l1 = _bf(lwc)
        rem = lwc - l1.astype(F32)
        l2 = _bf(rem)
        l3 = _bf(rem - l2.astype(F32))
        c = _dot(ltri, l1) + _dot(ltri, l2) + _dot(ltri, l3)
        c_last = c[L - 1:L, :]
        e_in = jnp.exp(c)
        e_prev = jnp.exp(c - lwc)
        e_out = jnp.exp(-c)
        e_end = jnp.exp(c_last - c)
        g_end = jnp.exp(c_last)
        rc = r_s[rows, :]
        kc = k_s[rows, :]
        vc = v_s[rows, :]
        kapc = kap_s[rows, :]
        betac = beta_s[rows, :]
        rt = rc * e_in
        kt = kc * e_out
        bt = betac * e_out
        kapt = kapc * e_prev
        khat = kc * e_end
        bhat = betac * e_end
        for pr in range(RW_W // LANES):
            sl = slice(LANES * pr, LANES * (pr + 1))
            rt_p = _bf(rt[:, sl])
            kts = _bf(stack(kt[:, sl]))
            bts = _bf(stack(bt[:, sl]))
            kapts = _bf(stack(kapt[:, sl]))
            vs = _bf(stack(vc[:, sl]))
            khats_t = _bf(stack(khat[:, sl]).T)
            bhats_t = _bf(stack(bhat[:, sl]).T)
            g_rows = jnp.broadcast_to(g_end[:, sl], (LANES, LANES)).T
            m_ab = jnp.where(strict, _dot_nt(kapts, bts), 0.0)
            m_ak = jnp.where(strict, _dot_nt(kapts, kts), 0.0)
            a_qk = jnp.where(incl, _dot_nt(rt_p, kts), 0.0)
            a_qb = jnp.where(incl, _dot_nt(rt_p, bts), 0.0)
            eye = jnp.where(r2 == c2, 1.0, 0.0)
            t_inv = eye - m_ab
            pw = _bf(m_ab)
            for it in range(5):
                pw_f = _dot(pw, pw)
                pw = _bf(pw_f)
                t_inv = t_inv + _dot(_bf(t_inv), pw)
            t_b = _bf(t_inv)
            w_mat = _bf(_dot(t_b, kapts))
            u0 = _dot(t_b, _bf(_dot(_bf(m_ak), vs)))
            y0 = _dot(_bf(a_qk), vs)
            kv0 = _dot(khats_t, vs)
            st = st_s[pr]
            st_b = _bf(st)
            u = _dot(w_mat, st_b) + u0
            u_b = _bf(u)
            y = _dot(rt_p, st_b) + y0 - _dot(_bf(a_qb), u_b)
            st_s[pr] = g_rows * st + kv0 - _dot(bhats_t, u_b)
            y_s[rows, sl] = y
        return carry

    lax.fori_loop(0, tb // L, chunk_body, 0)

    y = y_s[...]
    mean = segsum(y) * (1.0 / HEAD_DIM)
    d = y - mean
    var = segsum(d * d) * (1.0 / HEAD_DIM)
    yn = d * lax.rsqrt(var + LNX_EPS) * lng_ref[...] + lnb_ref[...]
    o_ref[0] = _bf((yn + bonus_s[...]) * g_s[...])


def _rwkv_call(x, g1, wrw, mu, wdp, wap, gup, w0, a0, kk, ka, rk, lng, lnb, ts):
    b, s, _ = x.shape
    const = lambda shape: pl.BlockSpec(shape, lambda i, j: (0,) * len(shape))
    row = const((1, RW_W))
    scr = lambda: pltpu.VMEM((ts, RW_W), F32)
    return pl.pallas_call(
        _rwkv_kernel,
        out_shape=jax.ShapeDtypeStruct((b, s, RW_W), BF16),
        grid=(b, s // ts),
        in_specs=[pl.BlockSpec((1, ts, D_MODEL), lambda i, j: (i, j, 0)),
                  const((1, D_MODEL)), const((D_MODEL, RW_COLS)), const((1, RW_COLS)),
                  const((LANES, RW_W)), const((LANES, RW_W)), const((GATE_LORA, RW_W)),
                  row, row, row, row, row, row, row],
        out_specs=pl.BlockSpec((1, ts, RW_W), lambda i, j: (i, j, 0)),
        scratch_shapes=[pltpu.VMEM((ts + SUBLANES, RW_COLS), F32),
                        scr(), scr(), scr(), scr(), scr(), scr(), scr(), scr(), scr(),
                        pltpu.VMEM((RW_W // LANES, LANES, LANES), F32)],
        compiler_params=pltpu.CompilerParams(dimension_semantics=("arbitrary", "arbitrary"),
                                             vmem_limit_bytes=VMEM_LIMIT),
        name="rwkv_mixer",
    )(x, g1, wrw, mu, wdp, wap, gup, w0, a0, kk, ka, rk, lng, lnb)


def _poolconv_kernel(x_ref, g1_ref, wpc_ref, poolw_ref, pscale_ref, convw_ref, ob_ref, oc_ref,
                     ubuf, vbuf):
    j = pl.program_id(1)
    tb = x_ref.shape[1]
    halo = POOL_MAXW

    @pl.when(j == 0)
    def _():
        ubuf[0:halo, :] = jnp.zeros((halo, POOL_W), F32)
        vbuf[0:SUBLANES, :] = jnp.zeros((SUBLANES, CONV_W), F32)

    h = _bf(_rmsnorm(x_ref[0], g1_ref[...]))
    p = _dot(h, wpc_ref[...])
    u = p[:, 0:POOL_W]
    ubuf[halo:halo + tb, :] = u
    t_glob = j * tb + _iota((tb, 1), 0)
    for gi, win in enumerate(POOL_WINDOWS):
        cs = slice(POOL_GW * gi, POOL_GW * (gi + 1))
        acc = u[:, cs]
        for dlt in range(1, win):
            acc = acc + ubuf[halo - dlt:halo - dlt + tb, cs]
        cnt = jnp.minimum(t_glob + 1, win).astype(F32)
        zc = acc / cnt - u[:, cs]
        zz = _dot(_bf(zc), poolw_ref[gi])
        ob_ref[0, :, cs] = _bf(zz * pscale_ref[:, cs])
    ubuf[0:halo, :] = ubuf[tb:tb + halo, :]

    bg = p[:, POOL_W:POOL_W + CONV_W]
    cg = p[:, POOL_W + CONV_W:POOL_W + 2 * CONV_W]
    cu = p[:, POOL_W + 2 * CONV_W:POOL_W + 3 * CONV_W]
    vv = cg * cu
    vbuf[SUBLANES:SUBLANES + tb, :] = vv
    conv = (convw_ref[0:1, :] * vbuf[SUBLANES - 2:SUBLANES - 2 + tb, :]
            + convw_ref[1:2, :] * vbuf[SUBLANES - 1:SUBLANES - 1 + tb, :]
            + convw_ref[2:3, :] * vv)
    vbuf[0:SUBLANES, :] = vbuf[tb:tb + SUBLANES, :]
    oc_ref[0] = _bf(bg * conv)


def _poolconv_call(x, g1, wpc, poolw, pscale, convw, ts):
    b, s, _ = x.shape
    const = lambda shape: pl.BlockSpec(shape, lambda i, j: (0,) * len(shape))
    out = jax.ShapeDtypeStruct((b, s, POOL_W), BF16)
    ospec = pl.BlockSpec((1, ts, POOL_W), lambda i, j: (i, j, 0))
    return pl.pallas_call(
        _poolconv_kernel,
        out_shape=(out, out),
        grid=(b, s // ts),
        in_specs=[pl.BlockSpec((1, ts, D_MODEL), lambda i, j: (i, j, 0)),
                  const((1, D_MODEL)), const((D_MODEL, POOL_W + 3 * CONV_W)),
                  const((len(POOL_WINDOWS), POOL_GW, POOL_GW)), const((1, POOL_W)),
                  const((CONV_K, CONV_W))],
        out_specs=(ospec, ospec),
        scratch_shapes=[pltpu.VMEM((ts + POOL_MAXW, POOL_W), F32),
                        pltpu.VMEM((ts + SUBLANES, CONV_W), F32)],
        compiler_params=pltpu.CompilerParams(dimension_semantics=("arbitrary", "arbitrary"),
                                             vmem_limit_bytes=VMEM_LIMIT),
        name="poolconv_mixer",
    )(x, g1, wpc, poolw, pscale, convw)


def _attn_kernel(x_ref, g1_ref, watt_ref, qg_ref, kg_ref, cos_ref, sin_ref, sink_ref, o_ref,
                 kbuf0, kbuf1, vbuf):
    j = pl.program_id(1)
    tq = x_ref.shape[1]
    blk = ATT_BLOCK
    nq = ATT_HQ * ATT_HD // LANES

    @pl.when(j == 0)
    def _():
        kbuf0[0:blk, :] = jnp.zeros((blk, LANES), BF16)
        kbuf1[0:blk, :] = jnp.zeros((blk, LANES), BF16)
        vbuf[0:blk, :] = jnp.zeros((blk, LANES), BF16)

    hi_ = (_iota((LANES, LANES), 0) % ATT_HD) // (ATT_HD // 2)
    hj_ = (_iota((LANES, LANES), 1) % ATT_HD) // (ATT_HD // 2)
    seg_mean = jnp.where(hi_ == hj_, 1.0 / ATT_HD, 0.0).astype(BF16)

    h = _bf(_rmsnorm(x_ref[0], g1_ref[...]))
    p = _dot(h, watt_ref[...])
    cos = cos_ref[0]
    sin = sin_ref[0]

    def norm_rope(xb, gain):
        ms = _dot_x2(xb * xb, seg_mean)
        yb = xb * lax.rsqrt(ms + NORM_EPS) * gain
        return yb * cos + pltpu.roll(yb, LANES // 2, axis=1) * sin

    kn = norm_rope(p[:, nq * LANES:(nq + 1) * LANES], kg_ref[...])
    kv_lane = (_iota((tq, LANES), 1) % ATT_HD) // (ATT_HD // 2)
    kbuf0[blk:blk + tq, :] = _bf(jnp.where(kv_lane == 0, kn, 0.0))
    kbuf1[blk:blk + tq, :] = _bf(jnp.where(kv_lane == 1, kn, 0.0))
    vbuf[blk:blk + tq, :] = _bf(p[:, (nq + 1) * LANES:(nq + 2) * LANES])
    qs = [_bf(norm_rope(p[:, LANES * b:LANES * (b + 1)], qg_ref[...]) * (ATT_HD ** -0.5))
          for b in range(nq)]

    rows = nq * blk
    qi = _iota((rows, 2 * blk), 0) % blk + blk
    kj = _iota((rows, 2 * blk), 1)
    dist = qi - kj
    band = (dist >= 0) & (dist < ATT_BLOCK)
    own = kj >= blk
    out_lo = _iota((rows, LANES), 1) < ATT_HD
    for n in range(tq // blk):
        q_st = jnp.concatenate([q[blk * n:blk * (n + 1), :] for q in qs], axis=0)
        vb = vbuf[blk * n:blk * (n + 2), :]
        has_prev = (j * (tq // blk) + n) > 0
        valid = band & (own | has_prev)
        outs = []
        for g, kbuf in enumerate((kbuf0, kbuf1)):
            kb = kbuf[blk * n:blk * (n + 2), :]
            sc = jnp.where(valid, _dot_nt(q_st, kb), -jnp.inf)
            sink = sink_ref[g]
            m = jnp.maximum(jnp.max(sc, axis=-1, keepdims=True), sink)
            e = jnp.exp(sc - m)
            den = jnp.sum(e, axis=-1, keepdims=True) + jnp.exp(sink - m)
            outs.append(_dot(_bf(e / den), vb))
        o = jnp.where(out_lo, outs[0], outs[1])
        for b in range(nq):
            o_ref[0, blk * n:blk * (n + 1), LANES * b:LANES * (b + 1)] = _bf(o[blk * b:blk * (b + 1), :])
    kbuf0[0:blk, :] = kbuf0[tq:tq + blk, :]
    kbuf1[0:blk, :] = kbuf1[tq:tq + blk, :]
    vbuf[0:blk, :] = vbuf[tq:tq + blk, :]


def _attn_call(x, g1, watt, qg, kg, cos, sin, sink_rows, ts):
    b, s, _ = x.shape
    const = lambda shape: pl.BlockSpec(shape, lambda i, j: (0,) * len(shape))
    tile = lambda w: pl.BlockSpec((1, ts, w), lambda i, j: (i, j, 0))
    return pl.pallas_call(
        _attn_kernel,
        out_shape=jax.ShapeDtypeStruct((b, s, ATT_HQ * ATT_HD), BF16),
        grid=(b, s // ts),
        in_specs=[tile(D_MODEL), const((1, D_MODEL)), const((D_MODEL, ATT_COLS)),
                  const((1, LANES)), const((1, LANES)), tile(LANES), tile(LANES),
                  const((ATT_HKV, ATT_G * ATT_BLOCK, 1))],
        out_specs=tile(ATT_HQ * ATT_HD),
        scratch_shapes=[pltpu.VMEM((ts + ATT_BLOCK, LANES), BF16),
                        pltpu.VMEM((ts + ATT_BLOCK, LANES), BF16),
                        pltpu.VMEM((ts + ATT_BLOCK, LANES), BF16)],
        compiler_params=pltpu.CompilerParams(dimension_semantics=("arbitrary", "arbitrary"),
                                             vmem_limit_bytes=VMEM_LIMIT),
        name="swa_attention",
    )(x, g1, watt, qg, kg, cos, sin, sink_rows)


def _merge_kernel(x_ref, g1_ref, wg_ref, za_ref, zb_ref, zc_ref, zd_ref,
                  wa_ref, wb_ref, wc_ref, wd_ref, wo_ref, o_ref):
    x = x_ref[...]
    h = _bf(_rmsnorm(x, g1_ref[...]))
    mixed = None
    for b, (z_ref, w_ref) in enumerate(((za_ref, wa_ref), (zb_ref, wb_ref), (zc_ref, wc_ref), (zd_ref, wd_ref))):
        gate = jax.nn.sigmoid(_dot(h, wg_ref[:, D_MODEL * b:D_MODEL * (b + 1)]))
        term = gate * _dot(z_ref[...], w_ref[...])
        mixed = term if mixed is None else mixed + term
    o_ref[...] = x + _dot(_bf(mixed), wo_ref[...])


def _merge_call(x2, g1, wg, za, zb, zc, zd, wa, wb, wc, wd, wo, tm):
    t = x2.shape[0]
    const = lambda shape: pl.BlockSpec(shape, lambda i: (0,) * len(shape))
    tile = lambda w: pl.BlockSpec((tm, w), lambda i: (i, 0))
    wout = const((RW_W, D_MODEL))
    return pl.pallas_call(
        _merge_kernel,
        out_shape=jax.ShapeDtypeStruct((t, D_MODEL), F32),
        grid=(t // tm,),
        in_specs=[tile(D_MODEL), const((1, D_MODEL)), const((D_MODEL, N_BRANCH * D_MODEL)),
                  tile(RW_W), tile(POOL_W), tile(CONV_W), tile(ATT_HQ * ATT_HD),
                  wout, wout, wout, wout, const((D_MODEL, D_MODEL))],
        out_specs=tile(D_MODEL),
        compiler_params=pltpu.CompilerParams(dimension_semantics=("arbitrary",),
                                             vmem_limit_bytes=VMEM_LIMIT),
        name="merge_mixers",
    )(x2, g1, wg, za, zb, zc, zd, wa, wb, wc, wd, wo)


def _ffn_kernel(x_ref, g2_ref, wg_ref, wu_ref, wd_ref, o_ref):
    x = x_ref[...]
    h = _bf(_rmsnorm(x, g2_ref[...]))
    acc = x
    for c in range(D_FF // FFN_CHUNK):
        cs = slice(FFN_CHUNK * c, FFN_CHUNK * (c + 1))
        gt = _dot(h, wg_ref[:, cs])
        up = _dot(h, wu_ref[:, cs])
        act = _bf(gt * jax.nn.sigmoid(gt) * up)
        acc = acc + _dot(act, wd_ref[cs, :])
    o_ref[...] = acc


def _ffn_call(x2, g2, wg, wu, wd, tm):
    t = x2.shape[0]
    const = lambda shape: pl.BlockSpec(shape, lambda i: (0,) * len(shape))
    tile = pl.BlockSpec((tm, D_MODEL), lambda i: (i, 0))
    return pl.pallas_call(
        _ffn_kernel,
        out_shape=jax.ShapeDtypeStruct((t, D_MODEL), F32),
        grid=(t // tm,),
        in_specs=[tile, const((1, D_MODEL)), const((D_MODEL, D_FF)), const((D_MODEL, D_FF)),
                  const((D_FF, D_MODEL))],
        out_specs=tile,
        compiler_params=pltpu.CompilerParams(dimension_semantics=("arbitrary",),
                                             vmem_limit_bytes=VMEM_LIMIT),
        name="ffn_swiglu",
    )(x2, g2, wg, wu, wd)


def _attn_perms():
    half = ATT_HD // 2
    q_cols = []
    for jb in range(ATT_G):
        for hf in range(2):
            for ab in range(ATT_HKV):
                head = ATT_G * ab + jb
                q_cols += [ATT_HD * head + half * hf + i for i in range(half)]
    k_cols = []
    for hf in range(2):
        for g in range(ATT_HKV):
            k_cols += [ATT_HD * g + half * hf + i for i in range(half)]
    gain_idx = [half * ((l % LANES) // (LANES // 2)) + l % half for l in range(LANES)]
    o_rows = []
    for jb in range(ATT_G):
        for ab in range(ATT_HKV):
            head = ATT_G * ab + jb
            o_rows += [ATT_HD * head + c for c in range(ATT_HD)]
    return q_cols, k_cols, gain_idx, o_rows


def kernel(x, positions, norm1_g, w_in, shift_mu, w_decay_up, w0, a_up, a0, g_up, k_k, k_a, r_k, lnx_g, lnx_b, w_rwkv_out, pool_w, pool_scale, w_pool_out, conv_w, w_conv_out, q_norm_g, k_norm_g, sinks, w_attn_out, w_o, norm2_g, w_ffn_gate, w_ffn_up, w_ffn_down):
    b, s, d = x.shape
    assert d == D_MODEL and s % ATT_BLOCK == 0
    ts = min(SEQ_TILE, s)
    tm = min(TOK_TILE, b * s)
    assert s % ts == 0 and (b * s) % tm == 0
    depth = w_in.shape[0]

    q_cols, k_cols, gain_idx, o_rows = _attn_perms()
    q_cols = jnp.asarray(q_cols, jnp.int32)
    k_cols = jnp.asarray(k_cols, jnp.int32)
    gain_idx = jnp.asarray(gain_idx, jnp.int32)
    o_rows = jnp.asarray(o_rows, jnp.int32)
    c_rw, c_pool, c_conv, c_att = RW_COLS, RW_COLS + POOL_W, RW_COLS + POOL_W + 3 * CONV_W, \
        RW_COLS + POOL_W + 3 * CONV_W + ATT_COLS

    cos, sin = _rope_tables(positions, ts)
    row = lambda v: v.reshape(1, -1).astype(F32)
    zeros_lora = jnp.zeros((DECAY_LORA, RW_W), F32)

    for i in range(depth):
        g1 = row(norm1_g[i])
        wi = w_in[i]
        wdp = _bf(jnp.concatenate([w_decay_up[i], zeros_lora], axis=0))
        wap = _bf(jnp.concatenate([zeros_lora, a_up[i]], axis=0))
        za = _rwkv_call(x, g1, _bf(wi[:, :c_rw]), row(shift_mu[i]), wdp, wap, _bf(g_up[i]),
                        row(w0[i]), row(a0[i]), row(k_k[i]), row(k_a[i]), row(r_k[i]),
                        row(lnx_g[i]), row(lnx_b[i]), ts)
        zb, zc = _poolconv_call(x, g1, _bf(wi[:, c_rw:c_conv]), _bf(pool_w[i]), row(pool_scale[i]),
                                conv_w[i].astype(F32), ts)
        w_att = wi[:, c_conv:c_att]
        w_att = jnp.concatenate([w_att[:, q_cols], w_att[:, ATT_HQ * ATT_HD + k_cols],
                                 w_att[:, (ATT_HQ + ATT_HKV) * ATT_HD:]], axis=1)
        sink_rows = jnp.repeat(sinks[i].astype(F32).reshape(ATT_HKV, ATT_G), ATT_BLOCK, axis=1)[..., None]
        zd = _attn_call(x, g1, _bf(w_att), row(q_norm_g[i][gain_idx]), row(k_norm_g[i][gain_idx]),
                        cos, sin, sink_rows, ts)
        flat = lambda z: z.reshape(b * s, z.shape[-1])
        x1 = _merge_call(flat(x), g1, _bf(wi[:, c_att:]), flat(za), flat(zb), flat(zc), flat(zd),
                         _bf(w_rwkv_out[i]), _bf(w_pool_out[i]), _bf(w_conv_out[i]),
                         _bf(w_attn_out[i][o_rows, :]), _bf(w_o[i]), tm)
        x2 = _ffn_call(x1, row(norm2_g[i]), _bf(w_ffn_gate[i]), _bf(w_ffn_up[i]), _bf(w_ffn_down[i]), tm)
        x = x2.reshape(b, s, d)
    return x
```

```python
import functools
import math

import jax
import jax.numpy as jnp
from jax import lax
from jax.experimental import pallas as pl
from jax.experimental.pallas import tpu as pltpu

F32 = jnp.float32
BF16 = jnp.bfloat16

D_MODEL = 1024
RW_HEADS = 8
HEAD_DIM = 64
RW_W = RW_HEADS * HEAD_DIM
DECAY_LORA = 64
ICLR_LORA = 64
GATE_LORA = 128
LNX_EPS = 64e-5
RW_COLS = 3 * RW_W + DECAY_LORA + ICLR_LORA + GATE_LORA
POOL_W = 512
POOL_GW = 128
POOL_WINDOWS = (2, 4, 8, 16)
POOL_MAXW = 16
CONV_W = 512
CONV_K = 3
ATT_HQ = 8
ATT_HKV = 2
ATT_G = ATT_HQ // ATT_HKV
ATT_HD = 64
ATT_BLOCK = 128
ATT_COLS = (ATT_HQ + 2 * ATT_HKV) * ATT_HD
ROPE_THETA = 10000.0
N_BRANCH = 4
D_FF = 2816
NORM_EPS = 1e-6

LANES = 128
SUBLANES = 8
WKV_CHUNK = 64
WKV_CHUNKS_PER_STEP = 2
ATT_BLOCKS_PER_STEP = 2
SEQ_TILE = 512
TOK_TILE = 512
FFN_CHUNK = 1408
VMEM_LIMIT = 48 * 1024 * 1024


def _bf(x):
    return x.astype(BF16)


def _dot(a, b):
    return jnp.dot(a, b, preferred_element_type=F32)


def _dot_nt(a, b):
    return lax.dot_general(a, b, (((1,), (1,)), ((), ())), preferred_element_type=F32)


def _split2(x):
    hi = _bf(x)
    lo = _bf(x - hi.astype(F32))
    return hi, lo


def _dot_x2(x, m):
    hi, lo = _split2(x)
    return _dot(hi, m) + _dot(lo, m)


def _rmsnorm(x, g):
    ms = jnp.mean(x * x, axis=-1, keepdims=True)
    return x * lax.rsqrt(ms + NORM_EPS) * g


def _iota(shape, dim):
    return lax.broadcasted_iota(jnp.int32, shape, dim)


def _rope_kernel(pos_ref, inv_ref, cos_ref, sin_ref):
    ang = pos_ref[0].astype(F32) * inv_ref[...]
    lane = _iota(ang.shape, 1)
    cos_ref[0] = jnp.cos(ang)
    s = jnp.sin(ang)
    sin_ref[0] = jnp.where(lane < LANES // 2, -s, s)


def _rope_tables(positions, ts):
    b, s = positions.shape
    half = ATT_HD // 2
    inv = ROPE_THETA ** (-jnp.arange(half, dtype=F32) * 2.0 / ATT_HD)
    inv = jnp.tile(inv, LANES // half)[None, :]
    pos3 = positions.reshape(b, s, 1)
    out = jax.ShapeDtypeStruct((b, s, LANES), F32)
    return pl.pallas_call(
        _rope_kernel,
        out_shape=(out, out),
        grid=(b, s // ts),
        in_specs=[pl.BlockSpec((1, ts, 1), lambda i, j: (i, j, 0)),
                  pl.BlockSpec((1, LANES), lambda i, j: (0, 0))],
        out_specs=(pl.BlockSpec((1, ts, LANES), lambda i, j: (i, j, 0)),
                   pl.BlockSpec((1, ts, LANES), lambda i, j: (i, j, 0))),
        compiler_params=pltpu.CompilerParams(dimension_semantics=("arbitrary", "arbitrary")),
        name="rope_tables",
    )(pos3, inv)


def _rwkv_kernel(x_ref, g1_ref, wrw_ref, mu_ref, wdp_ref, wap_ref, gup_ref, w0_ref, a0_ref,
                 kk_ref, ka_ref, rk_ref, lng_ref, lnb_ref, o_ref,
                 pbuf, r_s, k_s, v_s, kap_s, beta_s, lw_s, bonus_s, g_s, y_s, st_s):
    j = pl.program_id(1)
    tb = x_ref.shape[1]
    L = WKV_CHUNK

    @pl.when(j == 0)
    def _():
        st_s[...] = jnp.zeros(st_s.shape, F32)
        pbuf[0:SUBLANES, :] = jnp.zeros((SUBLANES, RW_COLS), F32)

    gi = _iota((LANES, LANES), 0) // HEAD_DIM
    gj = _iota((LANES, LANES), 1) // HEAD_DIM
    seg_ones = jnp.where(gi == gj, 1.0, 0.0).astype(BF16)

    def segsum(z):
        return jnp.concatenate(
            [_dot_x2(z[:, LANES * b:LANES * (b + 1)], seg_ones) for b in range(RW_W // LANES)], axis=1)

    x = x_ref[0]
    h = _bf(_rmsnorm(x, g1_ref[...]))
    p = _dot(h, wrw_ref[...])
    pbuf[SUBLANES:SUBLANES + tb, :] = p
    p_prev = pbuf[SUBLANES - 1:SUBLANES - 1 + tb, :]
    pbuf[SUBLANES - 1:SUBLANES, :] = p[tb - 1:tb, :]
    pm = p + (p_prev - p) * mu_ref[...]
    r = pm[:, 0:RW_W]
    k = pm[:, RW_W:2 * RW_W]
    v = pm[:, 2 * RW_W:3 * RW_W]
    lora_in = pm[:, 3 * RW_W:3 * RW_W + LANES]
    gd = pm[:, 3 * RW_W + LANES:RW_COLS]
    z = w0_ref[...] + _dot(_bf(jnp.tanh(lora_in)), wdp_ref[...])
    lw_s[...] = (-math.exp(-0.5)) * jax.nn.sigmoid(z)
    a = jax.nn.sigmoid(a0_ref[...] + _dot(_bf(lora_in), wap_ref[...]))
    g_s[...] = _dot(_bf(jax.nn.sigmoid(gd)), gup_ref[...])
    kk = k * kk_ref[...]
    kap = kk / jnp.maximum(jnp.sqrt(segsum(kk * kk)), 1e-12)
    k2 = k * (1.0 + (a - 1.0) * ka_ref[...])
    r_s[...] = r
    k_s[...] = k2
    v_s[...] = v
    kap_s[...] = kap
    beta_s[...] = kap * a
    bonus_s[...] = segsum(r * k2 * rk_ref[...]) * v

    lane_lo = _iota((L, LANES), 1) < HEAD_DIM
    r2 = _iota((2 * L, 2 * L), 0)
    c2 = _iota((2 * L, 2 * L), 1)
    strict = ((r2 // L) == (c2 // L)) & ((c2 % L) < (r2 % L))
    incl = (_iota((L, 2 * L), 1) % L) <= _iota((L, 2 * L), 0)
    ltri = jnp.where(_iota((L, L), 1) <= _iota((L, L), 0), 1.0, 0.0).astype(BF16)

    def stack(zz):
        return jnp.concatenate([jnp.where(lane_lo, zz, 0.0), jnp.where(lane_lo, 0.0, zz)], axis=0)

    eye = jnp.where(r2 == c2, 1.0, 0.0)
    n_pairs = RW_W // LANES
    n_ch = min(WKV_CHUNKS_PER_STEP, tb // L)

    def group_body(gidx, carry):
        chains = []
        for q in range(n_ch):
            t0 = pl.multiple_of((gidx * n_ch + q) * L, L)
            rows = pl.ds(t0, L)
            lwc = lw_s[rows, :]
            l1 = _bf(lwc)
            rem = lwc - l1.astype(F32)
            l2 = _bf(rem)
            l3 = _bf(rem - l2.astype(F32))
            c = _dot(ltri, l1) + _dot(ltri, l2) + _dot(ltri, l3)
            c_last = c[L - 1:L, :]
            e_in = jnp.exp(c)
            e_prev = jnp.exp(c - lwc)
            e_out = jnp.exp(-c)
            e_end = jnp.exp(c_last - c)
            g_end = jnp.exp(c_last)
            kc = k_s[rows, :]
            vc = v_s[rows, :]
            betac = beta_s[rows, :]
            rt = r_s[rows, :] * e_in
            kt = kc * e_out
            bt = betac * e_out
            kapt = kap_s[rows, :] * e_prev
            khat = kc * e_end
            bhat = betac * e_end
            for pr in range(n_pairs):
                sl = slice(LANES * pr, LANES * (pr + 1))
                chains.append(dict(
                    q=q, pr=pr, rows=rows, sl=sl,
                    rt=_bf(rt[:, sl]), kts=_bf(stack(kt[:, sl])), bts=_bf(stack(bt[:, sl])),
                    kapts=_bf(stack(kapt[:, sl])), vs=_bf(stack(vc[:, sl])),
                    khat=stack(khat[:, sl]), bhat=stack(bhat[:, sl]), g_end=g_end[:, sl]))
        for ch in chains:
            ch['m_ab'] = jnp.where(strict, _dot_nt(ch['kapts'], ch['bts']), 0.0)
        for ch in chains:
            ch['m_ak'] = _bf(jnp.where(strict, _dot_nt(ch['kapts'], ch['kts']), 0.0))
        for ch in chains:
            ch['t'] = eye - ch['m_ab']
            ch['pw'] = _bf(ch['m_ab'])
        for ch in chains:
            ch['pw'] = _bf(_dot(ch['pw'], ch['pw']))
        for ch in chains:
            ch['a_qk'] = _bf(jnp.where(incl, _dot_nt(ch['rt'], ch['kts']), 0.0))
        for ch in chains:
            ch['a_qb'] = _bf(jnp.where(incl, _dot_nt(ch['rt'], ch['bts']), 0.0))
        for it in range(5):
            if it < 4:
                for ch in chains:
                    ch['pw_next'] = _dot(ch['pw'], ch['pw'])
            for ch in chains:
                ch['t'] = ch['t'] + _dot(_bf(ch['t']), ch['pw'])
            if it == 0:
                for ch in chains:
                    ch['mv'] = _bf(_dot(ch['m_ak'], ch['vs']))
                for ch in chains:
                    ch['y0'] = _dot(ch['a_qk'], ch['vs'])
            if it == 1:
                for ch in chains:
                    ch['khats_t'] = _bf(ch['khat'].T)
                    ch['bhats_t'] = _bf(ch['bhat'].T)
                    ch['g_rows'] = jnp.broadcast_to(ch['g_end'], (LANES, LANES)).T
            if it == 2:
                for ch in chains:
                    ch['kv0'] = _dot(ch['khats_t'], ch['vs'])
            if it < 4:
                for ch in chains:
                    ch['pw'] = _bf(ch['pw_next'])
        for ch in chains:
            ch['t_b'] = _bf(ch['t'])
        for ch in chains:
            ch['w_mat'] = _bf(_dot(ch['t_b'], ch['kapts']))
        for ch in chains:
            ch['u0'] = _dot(ch['t_b'], ch['mv'])
        st = [st_s[pr] for pr in range(n_pairs)]
        for q in range(n_ch):
            cq = [ch for ch in chains if ch['q'] == q]
            st_b = [_bf(s_) for s_ in st]
            u_b = [_bf(_dot(ch['w_mat'], st_b[ch['pr']]) + ch['u0']) for ch in cq]
            st = [ch['g_rows'] * st[ch['pr']] + ch['kv0'] - _dot(ch['bhats_t'], u_b[ch['pr']]) for ch in cq]
            for ch in cq:
                y_s[ch['rows'], ch['sl']] = (_dot(ch['rt'], st_b[ch['pr']]) + ch['y0']
                                             - _dot(ch['a_qb'], u_b[ch['pr']]))
        for pr in range(n_pairs):
            st_s[pr] = st[pr]
        return carry

    lax.fori_loop(0, tb // (L * n_ch), group_body, 0)

    y = y_s[...]
    mean = segsum(y) * (1.0 / HEAD_DIM)
    d = y - mean
    var = segsum(d * d) * (1.0 / HEAD_DIM)
    yn = d * lax.rsqrt(var + LNX_EPS) * lng_ref[...] + lnb_ref[...]
    o_ref[0] = _bf((yn + bonus_s[...]) * g_s[...])


def _rwkv_call(x, g1, wrw, mu, wdp, wap, gup, w0, a0, kk, ka, rk, lng, lnb, ts):
    b, s, _ = x.shape
    const = lambda shape: pl.BlockSpec(shape, lambda i, j: (0,) * len(shape))
    row = const((1, RW_W))
    scr = lambda: pltpu.VMEM((ts, RW_W), F32)
    return pl.pallas_call(
        _rwkv_kernel,
        out_shape=jax.ShapeDtypeStruct((b, s, RW_W), BF16),
        grid=(b, s // ts),
        in_specs=[pl.BlockSpec((1, ts, D_MODEL), lambda i, j: (i, j, 0)),
                  const((1, D_MODEL)), const((D_MODEL, RW_COLS)), const((1, RW_COLS)),
                  const((LANES, RW_W)), const((LANES, RW_W)), const((GATE_LORA, RW_W)),
                  row, row, row, row, row, row, row],
        out_specs=pl.BlockSpec((1, ts, RW_W), lambda i, j: (i, j, 0)),
        scratch_shapes=[pltpu.VMEM((ts + SUBLANES, RW_COLS), F32),
                        scr(), scr(), scr(), scr(), scr(), scr(), scr(), scr(), scr(),
                        pltpu.VMEM((RW_W // LANES, LANES, LANES), F32)],
        compiler_params=pltpu.CompilerParams(dimension_semantics=("arbitrary", "arbitrary"),
                                             vmem_limit_bytes=VMEM_LIMIT),
        name="rwkv_mixer",
    )(x, g1, wrw, mu, wdp, wap, gup, w0, a0, kk, ka, rk, lng, lnb)


def _poolconv_kernel(x_ref, g1_ref, wpc_ref, poolw_ref, pscale_ref, convw_ref, ob_ref, oc_ref,
                     ubuf, vbuf):
    j = pl.program_id(1)
    tb = x_ref.shape[1]
    halo = POOL_MAXW

    @pl.when(j == 0)
    def _():
        ubuf[0:halo, :] = jnp.zeros((halo, POOL_W), F32)
        vbuf[0:SUBLANES, :] = jnp.zeros((SUBLANES, CONV_W), F32)

    h = _bf(_rmsnorm(x_ref[0], g1_ref[...]))
    p = _dot(h, wpc_ref[...])
    u = p[:, 0:POOL_W]
    ubuf[halo:halo + tb, :] = u
    t_glob = j * tb + _iota((tb, 1), 0)
    for gi, win in enumerate(POOL_WINDOWS):
        cs = slice(POOL_GW * gi, POOL_GW * (gi + 1))
        acc = u[:, cs]
        for dlt in range(1, win):
            acc = acc + ubuf[halo - dlt:halo - dlt + tb, cs]
        cnt = jnp.minimum(t_glob + 1, win).astype(F32)
        zc = acc / cnt - u[:, cs]
        zz = _dot(_bf(zc), poolw_ref[gi])
        ob_ref[0, :, cs] = _bf(zz * pscale_ref[:, cs])
    ubuf[0:halo, :] = ubuf[tb:tb + halo, :]

    bg = p[:, POOL_W:POOL_W + CONV_W]
    cg = p[:, POOL_W + CONV_W:POOL_W + 2 * CONV_W]
    cu = p[:, POOL_W + 2 * CONV_W:POOL_W + 3 * CONV_W]
    vv = cg * cu
    vbuf[SUBLANES:SUBLANES + tb, :] = vv
    conv = (convw_ref[0:1, :] * vbuf[SUBLANES - 2:SUBLANES - 2 + tb, :]
            + convw_ref[1:2, :] * vbuf[SUBLANES - 1:SUBLANES - 1 + tb, :]
            + convw_ref[2:3, :] * vv)
    vbuf[0:SUBLANES, :] = vbuf[tb:tb + SUBLANES, :]
    oc_ref[0] = _bf(bg * conv)


def _poolconv_call(x, g1, wpc, poolw, pscale, convw, ts):
    b, s, _ = x.shape
    const = lambda shape: pl.BlockSpec(shape, lambda i, j: (0,) * len(shape))
    out = jax.ShapeDtypeStruct((b, s, POOL_W), BF16)
    ospec = pl.BlockSpec((1, ts, POOL_W), lambda i, j: (i, j, 0))
    return pl.pallas_call(
        _poolconv_kernel,
        out_shape=(out, out),
        grid=(b, s // ts),
        in_specs=[pl.BlockSpec((1, ts, D_MODEL), lambda i, j: (i, j, 0)),
                  const((1, D_MODEL)), const((D_MODEL, POOL_W + 3 * CONV_W)),
                  const((len(POOL_WINDOWS), POOL_GW, POOL_GW)), const((1, POOL_W)),
                  const((CONV_K, CONV_W))],
        out_specs=(ospec, ospec),
        scratch_shapes=[pltpu.VMEM((ts + POOL_MAXW, POOL_W), F32),
                        pltpu.VMEM((ts + SUBLANES, CONV_W), F32)],
        compiler_params=pltpu.CompilerParams(dimension_semantics=("arbitrary", "arbitrary"),
                                             vmem_limit_bytes=VMEM_LIMIT),
        name="poolconv_mixer",
    )(x, g1, wpc, poolw, pscale, convw)


def _attn_kernel(x_ref, g1_ref, watt_ref, qg_ref, kg_ref, cos_ref, sin_ref, sink_ref, o_ref,
                 kbuf0, kbuf1, vbuf0, vbuf1, bias_tab):
    j = pl.program_id(1)
    tq = x_ref.shape[1]
    blk = ATT_BLOCK
    nq = ATT_HQ * ATT_HD // LANES
    rows = nq * blk

    @pl.when(j == 0)
    def _():
        kbuf0[0:blk, :] = jnp.zeros((blk, LANES), BF16)
        kbuf1[0:blk, :] = jnp.zeros((blk, LANES), BF16)
        vbuf0[0:blk, :] = jnp.zeros((blk, LANES), BF16)
        vbuf1[0:blk, :] = jnp.zeros((blk, LANES), BF16)
        qi = _iota((rows, 2 * blk), 0) % blk + blk
        kj = _iota((rows, 2 * blk), 1)
        dist = qi - kj
        band = (dist >= 0) & (dist < ATT_BLOCK)
        bias_tab[0] = jnp.where(band & (kj >= blk), 0.0, -jnp.inf)
        bias_tab[1] = jnp.where(band, 0.0, -jnp.inf)

    hi_ = (_iota((LANES, LANES), 0) % ATT_HD) // (ATT_HD // 2)
    hj_ = (_iota((LANES, LANES), 1) % ATT_HD) // (ATT_HD // 2)
    seg_mean = jnp.where(hi_ == hj_, 1.0 / ATT_HD, 0.0).astype(BF16)

    h = _bf(_rmsnorm(x_ref[0], g1_ref[...]))
    p = _dot(h, watt_ref[...])
    cos = cos_ref[0]
    sin = sin_ref[0]

    def norm_rope(xb, gain):
        ms = _dot_x2(xb * xb, seg_mean)
        yb = xb * lax.rsqrt(ms + NORM_EPS) * gain
        return yb * cos + pltpu.roll(yb, LANES // 2, axis=1) * sin

    kn = norm_rope(p[:, nq * LANES:(nq + 1) * LANES], kg_ref[...])
    kv_lane = (_iota((tq, LANES), 1) % ATT_HD) // (ATT_HD // 2)
    kbuf0[blk:blk + tq, :] = _bf(jnp.where(kv_lane == 0, kn, 0.0))
    kbuf1[blk:blk + tq, :] = _bf(jnp.where(kv_lane == 1, kn, 0.0))
    vv = p[:, (nq + 1) * LANES:(nq + 2) * LANES]
    v_lo = _iota((tq, LANES), 1) < ATT_HD
    vbuf0[blk:blk + tq, :] = _bf(jnp.where(v_lo, vv, 1.0))
    vbuf1[blk:blk + tq, :] = _bf(jnp.where(v_lo, 1.0, vv))
    qs = [_bf(norm_rope(p[:, LANES * b:LANES * (b + 1)], qg_ref[...]) * (ATT_HD ** -0.5))
          for b in range(nq)]

    out_lo = _iota((rows, LANES), 1) < ATT_HD
    first_tab = jnp.where(j == 0, 0, 1)
    nblk = tq // blk
    nstep = min(ATT_BLOCKS_PER_STEP, nblk)
    for n0 in range(0, nblk, nstep):
        chains = []
        for n in range(n0, n0 + nstep):
            q_st = jnp.concatenate([q[blk * n:blk * (n + 1), :] for q in qs], axis=0)
            for g, (kbuf, vbuf) in enumerate(((kbuf0, vbuf0), (kbuf1, vbuf1))):
                chains.append(dict(n=n, g=g, q=q_st, kb=kbuf[blk * n:blk * (n + 2), :],
                                   vb=vbuf[blk * n:blk * (n + 2), :], sink=sink_ref[g]))
        for ch in chains:
            bias = bias_tab[first_tab] if ch['n'] == 0 else bias_tab[1]
            ch['sc'] = _dot_nt(ch['q'], ch['kb']) + bias
        for ch in chains:
            ch['m'] = jnp.maximum(jnp.max(ch['sc'], axis=-1, keepdims=True), ch['sink'])
        for ch in chains:
            ch['e'] = _bf(jnp.exp(ch['sc'] - ch['m']))
        for ch in chains:
            ch['o'] = _dot(ch['e'], ch['vb'])
            ch['es'] = jnp.exp(ch['sink'] - ch['m'])
        for n in range(n0, n0 + nstep):
            c0, c1 = [ch for ch in chains if ch['n'] == n]
            num = jnp.where(out_lo, c0['o'], c1['o'])
            den = jnp.where(out_lo, pltpu.roll(c0['o'], ATT_HD, axis=1) + c0['es'],
                            pltpu.roll(c1['o'], ATT_HD, axis=1) + c1['es'])
            o = num / den
            for b in range(nq):
                o_ref[0, blk * n:blk * (n + 1), LANES * b:LANES * (b + 1)] = _bf(o[blk * b:blk * (b + 1), :])
    for buf in (kbuf0, kbuf1, vbuf0, vbuf1):
        buf[0:blk, :] = buf[tq:tq + blk, :]


def _attn_call(x, g1, watt, qg, kg, cos, sin, sink_rows, ts):
    b, s, _ = x.shape
    const = lambda shape: pl.BlockSpec(shape, lambda i, j: (0,) * len(shape))
    tile = lambda w: pl.BlockSpec((1, ts, w), lambda i, j: (i, j, 0))
    return pl.pallas_call(
        _attn_kernel,
        out_shape=jax.ShapeDtypeStruct((b, s, ATT_HQ * ATT_HD), BF16),
        grid=(b, s // ts),
        in_specs=[tile(D_MODEL), const((1, D_MODEL)), const((D_MODEL, ATT_COLS)),
                  const((1, LANES)), const((1, LANES)), tile(LANES), tile(LANES),
                  const((ATT_HKV, ATT_G * ATT_BLOCK, 1))],
        out_specs=tile(ATT_HQ * ATT_HD),
        scratch_shapes=[pltpu.VMEM((ts + ATT_BLOCK, LANES), BF16),
                        pltpu.VMEM((ts + ATT_BLOCK, LANES), BF16),
                        pltpu.VMEM((ts + ATT_BLOCK, LANES), BF16),
                        pltpu.VMEM((ts + ATT_BLOCK, LANES), BF16),
                        pltpu.VMEM((2, ATT_G * ATT_BLOCK, 2 * ATT_BLOCK), F32)],
        compiler_params=pltpu.CompilerParams(dimension_semantics=("arbitrary", "arbitrary"),
                                             vmem_limit_bytes=VMEM_LIMIT),
        name="swa_attention",
    )(x, g1, watt, qg, kg, cos, sin, sink_rows)


def _merge_kernel(x_ref, g1_ref, wg_ref, za_ref, zb_ref, zc_ref, zd_ref,
                  wa_ref, wb_ref, wc_ref, wd_ref, wo_ref, o_ref):
    x = x_ref[...]
    h = _bf(_rmsnorm(x, g1_ref[...]))
    mixed = None
    for b, (z_ref, w_ref) in enumerate(((za_ref, wa_ref), (zb_ref, wb_ref), (zc_ref, wc_ref), (zd_ref, wd_ref))):
        gate = jax.nn.sigmoid(_dot(h, wg_ref[:, D_MODEL * b:D_MODEL * (b + 1)]))
        term = gate * _dot(z_ref[...], w_ref[...])
        mixed = term if mixed is None else mixed + term
    o_ref[...] = x + _dot(_bf(mixed), wo_ref[...])


def _merge_call(x2, g1, wg, za, zb, zc, zd, wa, wb, wc, wd, wo, tm):
    t = x2.shape[0]
    const = lambda shape: pl.BlockSpec(shape, lambda i: (0,) * len(shape))
    tile = lambda w: pl.BlockSpec((tm, w), lambda i: (i, 0))
    wout = const((RW_W, D_MODEL))
    return pl.pallas_call(
        _merge_kernel,
        out_shape=jax.ShapeDtypeStruct((t, D_MODEL), F32),
        grid=(t // tm,),
        in_specs=[tile(D_MODEL), const((1, D_MODEL)), const((D_MODEL, N_BRANCH * D_MODEL)),
                  tile(RW_W), tile(POOL_W), tile(CONV_W), tile(ATT_HQ * ATT_HD),
                  wout, wout, wout, wout, const((D_MODEL, D_MODEL))],
        out_specs=tile(D_MODEL),
        compiler_params=pltpu.CompilerParams(dimension_semantics=("arbitrary",),
                                             vmem_limit_bytes=VMEM_LIMIT),
        name="merge_mixers",
    )(x2, g1, wg, za, zb, zc, zd, wa, wb, wc, wd, wo)


def _ffn_kernel(x_ref, g2_ref, wg_ref, wu_ref, wd_ref, o_ref):
    x = x_ref[...]
    h = _bf(_rmsnorm(x, g2_ref[...]))
    acc = x
    for c in range(D_FF // FFN_CHUNK):
        cs = slice(FFN_CHUNK * c, FFN_CHUNK * (c + 1))
        gt = _dot(h, wg_ref[:, cs])
        up = _dot(h, wu_ref[:, cs])
        act = _bf(gt * jax.nn.sigmoid(gt) * up)
        acc = acc + _dot(act, wd_ref[cs, :])
    o_ref[...] = acc


def _ffn_call(x2, g2, wg, wu, wd, tm):
    t = x2.shape[0]
    const = lambda shape: pl.BlockSpec(shape, lambda i: (0,) * len(shape))
    tile = pl.BlockSpec((tm, D_MODEL), lambda i: (i, 0))
    return pl.pallas_call(
        _ffn_kernel,
        out_shape=jax.ShapeDtypeStruct((t, D_MODEL), F32),
        grid=(t // tm,),
        in_specs=[tile, const((1, D_MODEL)), const((D_MODEL, D_FF)), const((D_MODEL, D_FF)),
                  const((D_FF, D_MODEL))],
        out_specs=tile,
        compiler_params=pltpu.CompilerParams(dimension_semantics=("arbitrary",),
                                             vmem_limit_bytes=VMEM_LIMIT),
        name="ffn_swiglu",
    )(x2, g2, wg, wu, wd)


def _attn_perms():
    half = ATT_HD // 2
    q_cols = []
    for jb in range(ATT_G):
        for hf in range(2):
            for ab in range(ATT_HKV):
                head = ATT_G * ab + jb
                q_cols += [ATT_HD * head + half * hf + i for i in range(half)]
    k_cols = []
    for hf in range(2):
        for g in range(ATT_HKV):
            k_cols += [ATT_HD * g + half * hf + i for i in range(half)]
    gain_idx = [half * ((l % LANES) // (LANES // 2)) + l % half for l in range(LANES)]
    o_rows = []
    for jb in range(ATT_G):
        for ab in range(ATT_HKV):
            head = ATT_G * ab + jb
            o_rows += [ATT_HD * head + c for c in range(ATT_HD)]
    return q_cols, k_cols, gain_idx, o_rows


def kernel(x, positions, norm1_g, w_in, shift_mu, w_decay_up, w0, a_up, a0, g_up, k_k, k_a, r_k, lnx_g, lnx_b, w_rwkv_out, pool_w, pool_scale, w_pool_out, conv_w, w_conv_out, q_norm_g, k_norm_g, sinks, w_attn_out, w_o, norm2_g, w_ffn_gate, w_ffn_up, w_ffn_down):
    b, s, d = x.shape
    assert d == D_MODEL and s % ATT_BLOCK == 0
    ts = min(SEQ_TILE, s)
    tm = min(TOK_TILE, b * s)
    assert s % ts == 0 and (b * s) % tm == 0
    depth = w_in.shape[0]

    q_cols, k_cols, gain_idx, o_rows = _attn_perms()
    q_cols = jnp.asarray(q_cols, jnp.int32)
    k_cols = jnp.asarray(k_cols, jnp.int32)
    gain_idx = jnp.asarray(gain_idx, jnp.int32)
    o_rows = jnp.asarray(o_rows, jnp.int32)
    c_rw, c_pool, c_conv, c_att = RW_COLS, RW_COLS + POOL_W, RW_COLS + POOL_W + 3 * CONV_W, \
        RW_COLS + POOL_W + 3 * CONV_W + ATT_COLS

    cos, sin = _rope_tables(positions, ts)
    row = lambda v: v.reshape(1, -1).astype(F32)
    zeros_lora = jnp.zeros((DECAY_LORA, RW_W), F32)

    for i in range(depth):
        g1 = row(norm1_g[i])
        wi = w_in[i]
        wdp = _bf(jnp.concatenate([w_decay_up[i], zeros_lora], axis=0))
        wap = _bf(jnp.concatenate([zeros_lora, a_up[i]], axis=0))
        za = _rwkv_call(x, g1, _bf(wi[:, :c_rw]), row(shift_mu[i]), wdp, wap, _bf(g_up[i]),
                        row(w0[i]), row(a0[i]), row(k_k[i]), row(k_a[i]), row(r_k[i]),
                        row(lnx_g[i]), row(lnx_b[i]), ts)
        zb, zc = _poolconv_call(x, g1, _bf(wi[:, c_rw:c_conv]), _bf(pool_w[i]), row(pool_scale[i]),
                                conv_w[i].astype(F32), ts)
        w_att = wi[:, c_conv:c_att]
        w_att = jnp.concatenate([w_att[:, q_cols], w_att[:, ATT_HQ * ATT_HD + k_cols],
                                 w_att[:, (ATT_HQ + ATT_HKV) * ATT_HD:]], axis=1)
        sink_rows = jnp.repeat(sinks[i].astype(F32).reshape(ATT_HKV, ATT_G), ATT_BLOCK, axis=1)[..., None]
        zd = _attn_call(x, g1, _bf(w_att), row(q_norm_g[i][gain_idx]), row(k_norm_g[i][gain_idx]),
                        cos, sin, sink_rows, ts)
        flat = lambda z: z.reshape(b * s, z.shape[-1])
        x1 = _merge_call(flat(x), g1, _bf(wi[:, c_att:]), flat(za), flat(zb), flat(zc), flat(zd),
                         _bf(w_rwkv_out[i]), _bf(w_pool_out[i]), _bf(w_conv_out[i]),
                         _bf(w_attn_out[i][o_rows, :]), _bf(w_o[i]), tm)
        x2 = _ffn_call(x1, row(norm2_g[i]), _bf(w_ffn_gate[i]), _bf(w_ffn_up[i]), _bf(w_ffn_down[i]), tm)
        x = x2.reshape(b, s, d)
    return x
```

```python
import functools
import math

import jax
import jax.numpy as jnp
from jax import lax
from jax.experimental import pallas as pl
from jax.experimental.pallas import tpu as pltpu

F32 = jnp.float32
BF16 = jnp.bfloat16

D_MODEL = 1024
RW_HEADS = 8
HEAD_DIM = 64
RW_W = RW_HEADS * HEAD_DIM
DECAY_LORA = 64
ICLR_LORA = 64
GATE_LORA = 128
LNX_EPS = 64e-5
RW_COLS = 3 * RW_W + DECAY_LORA + ICLR_LORA + GATE_LORA
POOL_W = 512
POOL_GW = 128
POOL_WINDOWS = (2, 4, 8, 16)
POOL_MAXW = 16
CONV_W = 512
CONV_K = 3
ATT_HQ = 8
ATT_HKV = 2
ATT_G = ATT_HQ // ATT_HKV
ATT_HD = 64
ATT_BLOCK = 128
ATT_COLS = (ATT_HQ + 2 * ATT_HKV) * ATT_HD
ROPE_THETA = 10000.0
N_BRANCH = 4
D_FF = 2816
NORM_EPS = 1e-6

LANES = 128
SUBLANES = 8
WKV_CHUNK = 64
WKV_CHUNKS_PER_STEP = 4
ATT_BLOCKS_PER_STEP = 2
SEQ_TILE = 512
TOK_TILE = 512
FFN_CHUNK = 1408
VMEM_LIMIT = 48 * 1024 * 1024


def _bf(x):
    return x.astype(BF16)


def _dot(a, b):
    return jnp.dot(a, b, preferred_element_type=F32)


def _dot_nt(a, b):
    return lax.dot_general(a, b, (((1,), (1,)), ((), ())), preferred_element_type=F32)


def _split2(x):
    hi = _bf(x)
    lo = _bf(x - hi.astype(F32))
    return hi, lo


def _dot_x2(x, m):
    hi, lo = _split2(x)
    return _dot(hi, m) + _dot(lo, m)


def _rmsnorm(x, g):
    ms = jnp.mean(x * x, axis=-1, keepdims=True)
    return x * lax.rsqrt(ms + NORM_EPS) * g


def _iota(shape, dim):
    return lax.broadcasted_iota(jnp.int32, shape, dim)


def _rope_kernel(pos_ref, inv_ref, cos_ref, sin_ref):
    ang = pos_ref[0].astype(F32) * inv_ref[...]
    lane = _iota(ang.shape, 1)
    cos_ref[0] = jnp.cos(ang)
    s = jnp.sin(ang)
    sin_ref[0] = jnp.where(lane < LANES // 2, -s, s)


def _rope_tables(positions, ts):
    b, s = positions.shape
    half = ATT_HD // 2
    inv = ROPE_THETA ** (-jnp.arange(half, dtype=F32) * 2.0 / ATT_HD)
    inv = jnp.tile(inv, LANES // half)[None, :]
    pos3 = positions.reshape(b, s, 1)
    out = jax.ShapeDtypeStruct((b, s, LANES), F32)
    return pl.pallas_call(
        _rope_kernel,
        out_shape=(out, out),
        grid=(b, s // ts),
        in_specs=[pl.BlockSpec((1, ts, 1), lambda i, j: (i, j, 0)),
                  pl.BlockSpec((1, LANES), lambda i, j: (0, 0))],
        out_specs=(pl.BlockSpec((1, ts, LANES), lambda i, j: (i, j, 0)),
                   pl.BlockSpec((1, ts, LANES), lambda i, j: (i, j, 0))),
        compiler_params=pltpu.CompilerParams(dimension_semantics=("arbitrary", "arbitrary")),
        name="rope_tables",
    )(pos3, inv)


def _rwkv_kernel(x_ref, g1_ref, wrw_ref, mu_ref, wdp_ref, wap_ref, gup_ref, w0_ref, a0_ref,
                 kk_ref, ka_ref, rk_ref, lng_ref, lnb_ref, o_ref,
                 pbuf, r_s, k_s, v_s, kap_s, beta_s, lw_s, bonus_s, g_s, y_s, st_s):
    j = pl.program_id(1)
    tb = x_ref.shape[1]
    L = WKV_CHUNK

    @pl.when(j == 0)
    def _():
        st_s[...] = jnp.zeros(st_s.shape, F32)
        pbuf[0:SUBLANES, :] = jnp.zeros((SUBLANES, RW_COLS), F32)

    gi = _iota((LANES, LANES), 0) // HEAD_DIM
    gj = _iota((LANES, LANES), 1) // HEAD_DIM
    seg_ones = jnp.where(gi == gj, 1.0, 0.0).astype(BF16)

    def segsum(z):
        return jnp.concatenate(
            [_dot(_bf(z[:, LANES * b:LANES * (b + 1)]), seg_ones) for b in range(RW_W // LANES)], axis=1)

    x = x_ref[0]
    h = _bf(_rmsnorm(x, g1_ref[...]))
    p = _dot(h, wrw_ref[...])
    pbuf[SUBLANES:SUBLANES + tb, :] = p
    p_prev = pbuf[SUBLANES - 1:SUBLANES - 1 + tb, :]
    pbuf[SUBLANES - 1:SUBLANES, :] = p[tb - 1:tb, :]
    pm = p + (p_prev - p) * mu_ref[...]
    r = pm[:, 0:RW_W]
    k = pm[:, RW_W:2 * RW_W]
    v = pm[:, 2 * RW_W:3 * RW_W]
    lora_in = pm[:, 3 * RW_W:3 * RW_W + LANES]
    gd = pm[:, 3 * RW_W + LANES:RW_COLS]
    z = w0_ref[...] + _dot(_bf(jnp.tanh(lora_in)), wdp_ref[...])
    lw_s[...] = (-math.exp(-0.5)) * jax.nn.sigmoid(z)
    a = jax.nn.sigmoid(a0_ref[...] + _dot(_bf(lora_in), wap_ref[...]))
    g_s[...] = _dot(_bf(jax.nn.sigmoid(gd)), gup_ref[...])
    kk = k * kk_ref[...]
    kap = kk * lax.rsqrt(jnp.maximum(segsum(kk * kk), 1e-24))
    k2 = k * (1.0 + (a - 1.0) * ka_ref[...])
    r_s[...] = r
    k_s[...] = k2
    v_s[...] = v
    kap_s[...] = kap
    beta_s[...] = kap * a
    bonus_s[...] = segsum(r * k2 * rk_ref[...]) * v

    assert L == HEAD_DIM
    lane_lo = _iota((L, LANES), 1) < HEAD_DIM
    tok = _iota((L, LANES), 0)
    col = _iota((L, LANES), 1) % L
    strict = col < tok
    incl = col <= tok
    eye_sbs = jnp.where(col == tok, 1.0, 0.0)
    same_head = (_iota((LANES, LANES), 0) // HEAD_DIM) == (_iota((LANES, LANES), 1) // HEAD_DIM)
    ltri = jnp.where(_iota((L, L), 1) <= _iota((L, L), 0), 1.0, 0.0).astype(BF16)

    def stack(zz):
        zero = jnp.zeros_like(zz)
        return jnp.concatenate([jnp.where(lane_lo, zz, zero), jnp.where(lane_lo, zero, zz)], axis=0)

    n_pairs = RW_W // LANES
    n_ch = min(WKV_CHUNKS_PER_STEP, tb // L)

    def phase1(gidx, chains):
        for q in range(n_ch):
            t0 = (gidx * n_ch + q) * L
            rows = slice(t0, t0 + L)
            lwc = lw_s[rows, :]
            l1 = _bf(lwc)
            rem = lwc - l1.astype(F32)
            l2 = _bf(rem)
            l3 = _bf(rem - l2.astype(F32))
            c = _dot(ltri, l1) + _dot(ltri, l2) + _dot(ltri, l3)
            c_last = c[L - 1:L, :]
            e_in = jnp.exp(c)
            e_prev = jnp.exp(c - lwc)
            e_out = jnp.exp(-c)
            e_end = jnp.exp(c_last - c)
            g_end = jnp.exp(c_last)
            kc = k_s[rows, :]
            vc = _bf(v_s[rows, :])
            betac = beta_s[rows, :]
            rt = _bf(r_s[rows, :] * e_in)
            kt = _bf(kc * e_out)
            bt = _bf(betac * e_out)
            kapt = _bf(kap_s[rows, :] * e_prev)
            khat = kc * e_end
            bhat = betac * e_end
            for pr in range(n_pairs):
                sl = slice(LANES * pr, LANES * (pr + 1))
                kts = stack(kt[:, sl])
                bts = stack(bt[:, sl])
                chains.append(dict(
                    q=q, pr=pr, rows=rows, sl=sl, rt=rt[:, sl], kapt=kapt[:, sl], v=vc[:, sl],
                    kb=jnp.concatenate([bts, kts], axis=0), kapts=stack(kapt[:, sl]),
                    vs=stack(vc[:, sl]), khat=khat[:, sl], bhat=bhat[:, sl], g_end=g_end[:, sl]))
            yield
        for ch in chains:
            ma = _dot_nt(jnp.concatenate([ch['kapt'], ch['rt']], axis=0), ch['kb'])
            ch['m_ab'] = jnp.where(strict, ma[:L, :LANES], 0.0)
            ch['m_ak'] = _bf(jnp.where(strict, ma[:L, LANES:], 0.0))
            ch['a_qb'] = _bf(jnp.where(incl, ma[L:, :LANES], 0.0))
            ch['a_qk'] = _bf(jnp.where(incl, ma[L:, LANES:], 0.0))
        yield
        for ch in chains:
            ch['t'] = eye_sbs - ch['m_ab']
            ch['pw'] = _bf(ch['m_ab'])
        for ch in chains:
            ch['pw'] = _bf(_dot(ch['pw'], stack(ch['pw'])))
        yield
        for it in range(5):
            for ch in chains:
                ch['pws'] = stack(ch['pw'])
            if it < 4:
                for ch in chains:
                    both = _dot(jnp.concatenate([ch['pw'], _bf(ch['t'])], axis=0), ch['pws'])
                    ch['pw_next'] = both[:L]
                    ch['t'] = ch['t'] + both[L:]
            else:
                for ch in chains:
                    ch['t'] = ch['t'] + _dot(_bf(ch['t']), ch['pws'])
            if it == 0:
                for ch in chains:
                    mvy = _dot(jnp.concatenate([ch['m_ak'], ch['a_qk']], axis=0), ch['vs'])
                    ch['mvs'] = stack(_bf(mvy[:L]))
                    ch['y0'] = mvy[L:]
            if it == 1:
                for ch in chains:
                    ch['khat_t'] = _bf(ch['khat'].T)
                    ch['bhat_t'] = _bf(ch['bhat'].T)
                    ch['g_rows'] = jnp.broadcast_to(ch['g_end'], (LANES, LANES)).T
            if it == 2:
                for ch in chains:
                    ch['kv0'] = _dot(ch['khat_t'], ch['v'])
            if it < 4:
                for ch in chains:
                    ch['pw'] = _bf(ch['pw_next'])
            yield
        for ch in chains:
            wu = _dot(_bf(ch['t']), jnp.concatenate([ch['kapts'], ch['mvs']], axis=1))
            ch['u0'] = wu[:, LANES:]
            ch['wr'] = jnp.concatenate([_bf(wu[:, :LANES]), ch['rt']], axis=0)
        yield

    def phase2(chains, st):
        for q in range(n_ch):
            cq = [ch for ch in chains if ch['q'] == q]
            st_b = [_bf(s_) for s_ in st]
            ws = [_dot(ch['wr'], st_b[ch['pr']]) for ch in cq]
            u_b = [_bf(ws[ch['pr']][:L] + ch['u0']) for ch in cq]
            yield
            st[:] = [ch['g_rows'] * st[ch['pr']]
                     + jnp.where(same_head, ch['kv0'] - _dot(ch['bhat_t'], u_b[ch['pr']]), 0.0) for ch in cq]
            for ch in cq:
                y_s[ch['rows'], ch['sl']] = (ws[ch['pr']][L:] + ch['y0']
                                             - _dot(ch['a_qb'], stack(u_b[ch['pr']])))
            yield

    def interleave(gens):
        while gens:
            gens = [g_ for g_ in gens if next(g_, 'done') != 'done']

    st = [st_s[pr] for pr in range(n_pairs)]
    prev = None
    for gidx in range(tb // (L * n_ch)):
        cur = []
        interleave([phase1(gidx, cur)] + ([phase2(prev, st)] if prev is not None else []))
        prev = cur
    interleave([phase2(prev, st)])
    for pr in range(n_pairs):
        st_s[pr] = st[pr]

    y = y_s[...]
    mean = segsum(y) * (1.0 / HEAD_DIM)
    d = y - mean
    var = segsum(d * d) * (1.0 / HEAD_DIM)
    yn = d * lax.rsqrt(var + LNX_EPS) * lng_ref[...] + lnb_ref[...]
    o_ref[0] = _bf((yn + bonus_s[...]) * g_s[...])


def _rwkv_call(x, g1, wrw, mu, wdp, wap, gup, w0, a0, kk, ka, rk, lng, lnb, ts):
    b, s, _ = x.shape
    const = lambda shape: pl.BlockSpec(shape, lambda i, j: (0,) * len(shape))
    row = const((1, RW_W))
    scr = lambda: pltpu.VMEM((ts, RW_W), F32)
    return pl.pallas_call(
        _rwkv_kernel,
        out_shape=jax.ShapeDtypeStruct((b, s, RW_W), BF16),
        grid=(b, s // ts),
        in_specs=[pl.BlockSpec((1, ts, D_MODEL), lambda i, j: (i, j, 0)),
                  const((1, D_MODEL)), const((D_MODEL, RW_COLS)), const((1, RW_COLS)),
                  const((LANES, RW_W)), const((LANES, RW_W)), const((GATE_LORA, RW_W)),
                  row, row, row, row, row, row, row],
        out_specs=pl.BlockSpec((1, ts, RW_W), lambda i, j: (i, j, 0)),
        scratch_shapes=[pltpu.VMEM((ts + SUBLANES, RW_COLS), F32),
                        scr(), scr(), scr(), scr(), scr(), scr(), scr(), scr(), scr(),
                        pltpu.VMEM((RW_W // LANES, LANES, LANES), F32)],
        compiler_params=pltpu.CompilerParams(dimension_semantics=("arbitrary", "arbitrary"),
                                             vmem_limit_bytes=VMEM_LIMIT),
        name="rwkv_mixer",
    )(x, g1, wrw, mu, wdp, wap, gup, w0, a0, kk, ka, rk, lng, lnb)


def _poolconv_kernel(x_ref, g1_ref, wpc_ref, poolw_ref, pscale_ref, convw_ref, ob_ref, oc_ref,
                     ubuf, vbuf):
    j = pl.program_id(1)
    tb = x_ref.shape[1]
    halo = POOL_MAXW

    @pl.when(j == 0)
    def _():
        ubuf[0:halo, :] = jnp.zeros((halo, POOL_W), F32)
        vbuf[0:SUBLANES, :] = jnp.zeros((SUBLANES, CONV_W), F32)

    h = _bf(_rmsnorm(x_ref[0], g1_ref[...]))
    p = _dot(h, wpc_ref[...])
    u = p[:, 0:POOL_W]
    ubuf[halo:halo + tb, :] = u
    t_glob = j * tb + _iota((tb, 1), 0)
    for gi, win in enumerate(POOL_WINDOWS):
        cs = slice(POOL_GW * gi, POOL_GW * (gi + 1))
        acc = u[:, cs]
        for dlt in range(1, win):
            acc = acc + ubuf[halo - dlt:halo - dlt + tb, cs]
        cnt = jnp.minimum(t_glob + 1, win).astype(F32)
        zc = acc / cnt - u[:, cs]
        zz = _dot(_bf(zc), poolw_ref[gi])
        ob_ref[0, :, cs] = _bf(zz * pscale_ref[:, cs])
    ubuf[0:halo, :] = ubuf[tb:tb + halo, :]

    bg = p[:, POOL_W:POOL_W + CONV_W]
    cg = p[:, POOL_W + CONV_W:POOL_W + 2 * CONV_W]
    cu = p[:, POOL_W + 2 * CONV_W:POOL_W + 3 * CONV_W]
    vv = cg * cu
    vbuf[SUBLANES:SUBLANES + tb, :] = vv
    conv = (convw_ref[0:1, :] * vbuf[SUBLANES - 2:SUBLANES - 2 + tb, :]
            + convw_ref[1:2, :] * vbuf[SUBLANES - 1:SUBLANES - 1 + tb, :]
            + convw_ref[2:3, :] * vv)
    vbuf[0:SUBLANES, :] = vbuf[tb:tb + SUBLANES, :]
    oc_ref[0] = _bf(bg * conv)


def _poolconv_call(x, g1, wpc, poolw, pscale, convw, ts):
    b, s, _ = x.shape
    const = lambda shape: pl.BlockSpec(shape, lambda i, j: (0,) * len(shape))
    out = jax.ShapeDtypeStruct((b, s, POOL_W), BF16)
    ospec = pl.BlockSpec((1, ts, POOL_W), lambda i, j: (i, j, 0))
    return pl.pallas_call(
        _poolconv_kernel,
        out_shape=(out, out),
        grid=(b, s // ts),
        in_specs=[pl.BlockSpec((1, ts, D_MODEL), lambda i, j: (i, j, 0)),
                  const((1, D_MODEL)), const((D_MODEL, POOL_W + 3 * CONV_W)),
                  const((len(POOL_WINDOWS), POOL_GW, POOL_GW)), const((1, POOL_W)),
                  const((CONV_K, CONV_W))],
        out_specs=(ospec, ospec),
        scratch_shapes=[pltpu.VMEM((ts + POOL_MAXW, POOL_W), F32),
                        pltpu.VMEM((ts + SUBLANES, CONV_W), F32)],
        compiler_params=pltpu.CompilerParams(dimension_semantics=("arbitrary", "arbitrary"),
                                             vmem_limit_bytes=VMEM_LIMIT),
        name="poolconv_mixer",
    )(x, g1, wpc, poolw, pscale, convw)


def _attn_kernel(x_ref, g1_ref, watt_ref, qg_ref, kg_ref, cos_ref, sin_ref, sink_ref, o_ref,
                 kbuf0, kbuf1, vbuf, bias_tab):
    j = pl.program_id(1)
    tq = x_ref.shape[1]
    blk = ATT_BLOCK
    nq = ATT_HQ * ATT_HD // LANES
    rows = nq * blk

    @pl.when(j == 0)
    def _():
        kbuf0[0:blk, :] = jnp.zeros((blk, LANES), BF16)
        kbuf1[0:blk, :] = jnp.zeros((blk, LANES), BF16)
        vbuf[0:blk, :] = jnp.zeros((blk, LANES), BF16)
        qi = _iota((rows, 2 * blk), 0) % blk + blk
        kj = _iota((rows, 2 * blk), 1)
        dist = qi - kj
        band = (dist >= 0) & (dist < ATT_BLOCK)
        for g in range(ATT_HKV):
            sink = sink_ref[g]
            bias_tab[0, g] = jnp.where(kj == 0, sink, jnp.where(band & (kj >= blk), 0.0, -jnp.inf))
            bias_tab[1, g] = jnp.where(kj == 0, sink, jnp.where(band, 0.0, -jnp.inf))

    hi_ = (_iota((LANES, LANES), 0) % ATT_HD) // (ATT_HD // 2)
    hj_ = (_iota((LANES, LANES), 1) % ATT_HD) // (ATT_HD // 2)
    seg_mean = jnp.where(hi_ == hj_, 1.0 / ATT_HD, 0.0).astype(BF16)

    h = _bf(_rmsnorm(x_ref[0], g1_ref[...]))
    p = _dot(h, watt_ref[...])
    cos = cos_ref[0]
    sin = sin_ref[0]

    def norm_rope(xb, gain):
        ms = _dot_x2(xb * xb, seg_mean)
        yb = xb * lax.rsqrt(ms + NORM_EPS) * gain
        return yb * cos + pltpu.roll(yb, LANES // 2, axis=1) * sin

    kn = norm_rope(p[:, nq * LANES:(nq + 1) * LANES], kg_ref[...])
    kv_lane = (_iota((tq, LANES), 1) % ATT_HD) // (ATT_HD // 2)
    kbuf0[blk:blk + tq, :] = _bf(jnp.where(kv_lane == 0, kn, 0.0))
    kbuf1[blk:blk + tq, :] = _bf(jnp.where(kv_lane == 1, kn, 0.0))
    vbuf[blk:blk + tq, :] = _bf(p[:, (nq + 1) * LANES:(nq + 2) * LANES])
    qs = [_bf(norm_rope(p[:, LANES * b:LANES * (b + 1)], qg_ref[...]) * (ATT_HD ** -0.5))
          for b in range(nq)]

    out_lo = _iota((rows, LANES), 1) < ATT_HD
    not_sink_row = _iota((2 * blk, LANES), 0) > 0
    ones_cols = jnp.ones((2 * blk, LANES), BF16)
    zero_kv = jnp.zeros((2 * blk, LANES), BF16)
    first_tab = jnp.where(j == 0, 0, 1)
    nblk = tq // blk
    nstep = min(ATT_BLOCKS_PER_STEP, nblk)
    for n0 in range(0, nblk, nstep):
        chains = []
        for n in range(n0, n0 + nstep):
            q_st = jnp.concatenate([q[blk * n:blk * (n + 1), :] for q in qs], axis=0)
            win = slice(blk * n, blk * (n + 2))
            v1 = jnp.concatenate([jnp.where(not_sink_row, vbuf[win, :], zero_kv), ones_cols], axis=1)
            for g, kbuf in enumerate((kbuf0, kbuf1)):
                chains.append(dict(n=n, g=g, q=q_st, v1=v1,
                                   kb=jnp.where(not_sink_row, kbuf[win, :], zero_kv)))
        for ch in chains:
            bias = bias_tab[first_tab, ch['g']] if ch['n'] == 0 else bias_tab[1, ch['g']]
            ch['sc'] = _dot_nt(ch['q'], ch['kb']) + bias
        for ch in chains:
            ch['m'] = jnp.max(ch['sc'], axis=-1, keepdims=True)
        for ch in chains:
            ch['e'] = _bf(jnp.exp(ch['sc'] - ch['m']))
        for ch in chains:
            ch['o'] = _dot(ch['e'], ch['v1'])
        for n in range(n0, n0 + nstep):
            c0, c1 = [ch for ch in chains if ch['n'] == n]
            num = jnp.where(out_lo, c0['o'][:, :LANES], c1['o'][:, :LANES])
            den = jnp.where(out_lo, c0['o'][:, LANES:], c1['o'][:, LANES:])
            o = num / den
            for b in range(nq):
                o_ref[0, blk * n:blk * (n + 1), LANES * b:LANES * (b + 1)] = _bf(o[blk * b:blk * (b + 1), :])
    for buf in (kbuf0, kbuf1, vbuf):
        buf[0:blk, :] = buf[tq:tq + blk, :]


def _attn_call(x, g1, watt, qg, kg, cos, sin, sink_rows, ts):
    b, s, _ = x.shape
    const = lambda shape: pl.BlockSpec(shape, lambda i, j: (0,) * len(shape))
    tile = lambda w: pl.BlockSpec((1, ts, w), lambda i, j: (i, j, 0))
    return pl.pallas_call(
        _attn_kernel,
        out_shape=jax.ShapeDtypeStruct((b, s, ATT_HQ * ATT_HD), BF16),
        grid=(b, s // ts),
        in_specs=[tile(D_MODEL), const((1, D_MODEL)), const((D_MODEL, ATT_COLS)),
                  const((1, LANES)), const((1, LANES)), tile(LANES), tile(LANES),
                  const((ATT_HKV, ATT_G * ATT_BLOCK, 1))],
        out_specs=tile(ATT_HQ * ATT_HD),
        scratch_shapes=[pltpu.VMEM((ts + ATT_BLOCK, LANES), BF16),
                        pltpu.VMEM((ts + ATT_BLOCK, LANES), BF16),
                        pltpu.VMEM((ts + ATT_BLOCK, LANES), BF16),
                        pltpu.VMEM((2, ATT_HKV, ATT_G * ATT_BLOCK, 2 * ATT_BLOCK), F32)],
        compiler_params=pltpu.CompilerParams(dimension_semantics=("arbitrary", "arbitrary"),
                                             vmem_limit_bytes=VMEM_LIMIT),
        name="swa_attention",
    )(x, g1, watt, qg, kg, cos, sin, sink_rows)


def _merge_kernel(x_ref, g1_ref, wg_ref, za_ref, zb_ref, zc_ref, zd_ref,
                  wa_ref, wb_ref, wc_ref, wd_ref, wo_ref, o_ref):
    x = x_ref[...]
    h = _bf(_rmsnorm(x, g1_ref[...]))
    mixed = None
    for b, (z_ref, w_ref) in enumerate(((za_ref, wa_ref), (zb_ref, wb_ref), (zc_ref, wc_ref), (zd_ref, wd_ref))):
        gate = jax.nn.sigmoid(_dot(h, wg_ref[:, D_MODEL * b:D_MODEL * (b + 1)]))
        term = gate * _dot(z_ref[...], w_ref[...])
        mixed = term if mixed is None else mixed + term
    o_ref[...] = x + _dot(_bf(mixed), wo_ref[...])


def _merge_call(x2, g1, wg, za, zb, zc, zd, wa, wb, wc, wd, wo, tm):
    t = x2.shape[0]
    const = lambda shape: pl.BlockSpec(shape, lambda i: (0,) * len(shape))
    tile = lambda w: pl.BlockSpec((tm, w), lambda i: (i, 0))
    wout = const((RW_W, D_MODEL))
    return pl.pallas_call(
        _merge_kernel,
        out_shape=jax.ShapeDtypeStruct((t, D_MODEL), F32),
        grid=(t // tm,),
        in_specs=[tile(D_MODEL), const((1, D_MODEL)), const((D_MODEL, N_BRANCH * D_MODEL)),
                  tile(RW_W), tile(POOL_W), tile(CONV_W), tile(ATT_HQ * ATT_HD),
                  wout, wout, wout, wout, const((D_MODEL, D_MODEL))],
        out_specs=tile(D_MODEL),
        compiler_params=pltpu.CompilerParams(dimension_semantics=("arbitrary",),
                                             vmem_limit_bytes=VMEM_LIMIT),
        name="merge_mixers",
    )(x2, g1, wg, za, zb, zc, zd, wa, wb, wc, wd, wo)


def _ffn_kernel(x_ref, g2_ref, wg_ref, wu_ref, wd_ref, o_ref):
    x = x_ref[...]
    h = _bf(_rmsnorm(x, g2_ref[...]))
    acc = x
    for c in range(D_FF // FFN_CHUNK):
        cs = slice(FFN_CHUNK * c, FFN_CHUNK * (c + 1))
        gt = _dot(h, wg_ref[:, cs])
        up = _dot(h, wu_ref[:, cs])
        act = _bf(gt * jax.nn.sigmoid(gt) * up)
        acc = acc + _dot(act, wd_ref[cs, :])
    o_ref[...] = acc


def _ffn_call(x2, g2, wg, wu, wd, tm):
    t = x2.shape[0]
    const = lambda shape: pl.BlockSpec(shape, lambda i: (0,) * len(shape))
    tile = pl.BlockSpec((tm, D_MODEL), lambda i: (i, 0))
    return pl.pallas_call(
        _ffn_kernel,
        out_shape=jax.ShapeDtypeStruct((t, D_MODEL), F32),
        grid=(t // tm,),
        in_specs=[tile, const((1, D_MODEL)), const((D_MODEL, D_FF)), const((D_MODEL, D_FF)),
                  const((D_FF, D_MODEL))],
        out_specs=tile,
        compiler_params=pltpu.CompilerParams(dimension_semantics=("arbitrary",),
                                             vmem_limit_bytes=VMEM_LIMIT),
        name="ffn_swiglu",
    )(x2, g2, wg, wu, wd)


def _attn_perms():
    half = ATT_HD // 2
    q_cols = []
    for jb in range(ATT_G):
        for hf in range(2):
            for ab in range(ATT_HKV):
                head = ATT_G * ab + jb
                q_cols += [ATT_HD * head + half * hf + i for i in range(half)]
    k_cols = []
    for hf in range(2):
        for g in range(ATT_HKV):
            k_cols += [ATT_HD * g + half * hf + i for i in range(half)]
    gain_idx = [half * ((l % LANES) // (LANES // 2)) + l % half for l in range(LANES)]
    o_rows = []
    for jb in range(ATT_G):
        for ab in range(ATT_HKV):
            head = ATT_G * ab + jb
            o_rows += [ATT_HD * head + c for c in range(ATT_HD)]
    return q_cols, k_cols, gain_idx, o_rows


def kernel(x, positions, norm1_g, w_in, shift_mu, w_decay_up, w0, a_up, a0, g_up, k_k, k_a, r_k, lnx_g, lnx_b, w_rwkv_out, pool_w, pool_scale, w_pool_out, conv_w, w_conv_out, q_norm_g, k_norm_g, sinks, w_attn_out, w_o, norm2_g, w_ffn_gate, w_ffn_up, w_ffn_down):
    b, s, d = x.shape
    assert d == D_MODEL and s % ATT_BLOCK == 0
    ts = min(SEQ_TILE, s)
    tm = min(TOK_TILE, b * s)
    assert s % ts == 0 and (b * s) % tm == 0
    depth = w_in.shape[0]

    q_cols, k_cols, gain_idx, o_rows = _attn_perms()
    q_cols = jnp.asarray(q_cols, jnp.int32)
    k_cols = jnp.asarray(k_cols, jnp.int32)
    gain_idx = jnp.asarray(gain_idx, jnp.int32)
    o_rows = jnp.asarray(o_rows, jnp.int32)
    c_rw, c_pool, c_conv, c_att = RW_COLS, RW_COLS + POOL_W, RW_COLS + POOL_W + 3 * CONV_W, \
        RW_COLS + POOL_W + 3 * CONV_W + ATT_COLS

    cos, sin = _rope_tables(positions, ts)
    row = lambda v: v.reshape(1, -1).astype(F32)
    zeros_lora = jnp.zeros((DECAY_LORA, RW_W), F32)

    for i in range(depth):
        g1 = row(norm1_g[i])
        wi = w_in[i]
        wdp = _bf(jnp.concatenate([w_decay_up[i], zeros_lora], axis=0))
        wap = _bf(jnp.concatenate([zeros_lora, a_up[i]], axis=0))
        za = _rwkv_call(x, g1, _bf(wi[:, :c_rw]), row(shift_mu[i]), wdp, wap, _bf(g_up[i]),
                        row(w0[i]), row(a0[i]), row(k_k[i]), row(k_a[i]), row(r_k[i]),
                        row(lnx_g[i]), row(lnx_b[i]), ts)
        zb, zc = _poolconv_call(x, g1, _bf(wi[:, c_rw:c_conv]), _bf(pool_w[i]), row(pool_scale[i]),
                                conv_w[i].astype(F32), ts)
        w_att = wi[:, c_conv:c_att]
        w_att = jnp.concatenate([w_att[:, q_cols], w_att[:, ATT_HQ * ATT_HD + k_cols],
                                 w_att[:, (ATT_HQ + ATT_HKV) * ATT_HD:]], axis=1)
        sink_rows = jnp.repeat(sinks[i].astype(F32).reshape(ATT_HKV, ATT_G), ATT_BLOCK, axis=1)[..., None]
        zd = _attn_call(x, g1, _bf(w_att), row(q_norm_g[i][gain_idx]), row(k_norm_g[i][gain_idx]),
                        cos, sin, sink_rows, ts)
        flat = lambda z: z.reshape(b * s, z.shape[-1])
        x1 = _merge_call(flat(x), g1, _bf(wi[:, c_att:]), flat(za), flat(zb), flat(zc), flat(zd),
                         _bf(w_rwkv_out[i]), _bf(w_pool_out[i]), _bf(w_conv_out[i]),
                         _bf(w_attn_out[i][o_rows, :]), _bf(w_o[i]), tm)
        x2 = _ffn_call(x1, row(norm2_g[i]), _bf(w_ffn_gate[i]), _bf(w_ffn_up[i]), _bf(w_ffn_down[i]), tm)
        x = x2.reshape(b, s, d)
    return x
```

```python
import functools
import math

import jax
import jax.numpy as jnp
from jax import lax
from jax.experimental import pallas as pl
from jax.experimental.pallas import tpu as pltpu

F32 = jnp.float32
BF16 = jnp.bfloat16

D_MODEL = 1024
RW_HEADS = 8
HEAD_DIM = 64
RW_W = RW_HEADS * HEAD_DIM
DECAY_LORA = 64
ICLR_LORA = 64
GATE_LORA = 128
LNX_EPS = 64e-5
RW_COLS = 3 * RW_W + DECAY_LORA + ICLR_LORA + GATE_LORA
POOL_W = 512
POOL_GW = 128
POOL_WINDOWS = (2, 4, 8, 16)
POOL_MAXW = 16
CONV_W = 512
CONV_K = 3
ATT_HQ = 8
ATT_HKV = 2
ATT_G = ATT_HQ // ATT_HKV
ATT_HD = 64
ATT_BLOCK = 128
ATT_COLS = (ATT_HQ + 2 * ATT_HKV) * ATT_HD
ROPE_THETA = 10000.0
N_BRANCH = 4
D_FF = 2816
NORM_EPS = 1e-6

LANES = 128
SUBLANES = 8
WKV_CHUNK = 64
WKV_CHUNKS_PER_STEP = 4
ATT_BLOCKS_PER_STEP = 2
SEQ_TILE = 512
TOK_TILE = 512
FFN_CHUNK = 256
VMEM_LIMIT = 48 * 1024 * 1024


def _bf(x):
    return x.astype(BF16)


def _dot(a, b):
    return jnp.dot(a, b, preferred_element_type=F32)


def _dot_nt(a, b):
    return lax.dot_general(a, b, (((1,), (1,)), ((), ())), preferred_element_type=F32)


def _split2(x):
    hi = _bf(x)
    lo = _bf(x - hi.astype(F32))
    return hi, lo


def _dot_x2(x, m):
    hi, lo = _split2(x)
    return _dot(hi, m) + _dot(lo, m)


def _rmsnorm(x, g):
    ms = jnp.mean(x * x, axis=-1, keepdims=True)
    return x * lax.rsqrt(ms + NORM_EPS) * g


def _iota(shape, dim):
    return lax.broadcasted_iota(jnp.int32, shape, dim)


def _rope_kernel(pos_ref, inv_ref, cos_ref, sin_ref):
    ang = pos_ref[0].astype(F32) * inv_ref[...]
    lane = _iota(ang.shape, 1)
    cos_ref[0] = jnp.cos(ang)
    s = jnp.sin(ang)
    sin_ref[0] = jnp.where(lane < LANES // 2, -s, s)


def _rope_tables(positions, ts):
    b, s = positions.shape
    half = ATT_HD // 2
    inv = ROPE_THETA ** (-jnp.arange(half, dtype=F32) * 2.0 / ATT_HD)
    inv = jnp.tile(inv, LANES // half)[None, :]
    pos3 = positions.reshape(b, s, 1)
    out = jax.ShapeDtypeStruct((b, s, LANES), F32)
    return pl.pallas_call(
        _rope_kernel,
        out_shape=(out, out),
        grid=(b, s // ts),
        in_specs=[pl.BlockSpec((1, ts, 1), lambda i, j: (i, j, 0)),
                  pl.BlockSpec((1, LANES), lambda i, j: (0, 0))],
        out_specs=(pl.BlockSpec((1, ts, LANES), lambda i, j: (i, j, 0)),
                   pl.BlockSpec((1, ts, LANES), lambda i, j: (i, j, 0))),
        compiler_params=pltpu.CompilerParams(dimension_semantics=("arbitrary", "arbitrary")),
        name="rope_tables",
    )(pos3, inv)


def _rwkv_kernel(x_ref, g1_ref, wrw_ref, mu_ref, wdp_ref, wap_ref, gup_ref, w0_ref, a0_ref,
                 kk_ref, ka_ref, rk_ref, lng_ref, lnb_ref, o_ref,
                 pbuf, r_s, k_s, v_s, kap_s, beta_s, lw_s, bonus_s, g_s, y_s, st_s):
    j = pl.program_id(1)
    tb = x_ref.shape[1]
    L = WKV_CHUNK

    @pl.when(j == 0)
    def _():
        st_s[...] = jnp.zeros(st_s.shape, F32)
        pbuf[0:SUBLANES, :] = jnp.zeros((SUBLANES, RW_COLS), F32)

    gi = _iota((LANES, LANES), 0) // HEAD_DIM
    gj = _iota((LANES, LANES), 1) // HEAD_DIM
    seg_ones = jnp.where(gi == gj, 1.0, 0.0).astype(BF16)

    def segsum(z):
        return jnp.concatenate(
            [_dot(_bf(z[:, LANES * b:LANES * (b + 1)]), seg_ones) for b in range(RW_W // LANES)], axis=1)

    x = x_ref[0]
    h = _bf(_rmsnorm(x, g1_ref[...]))
    p = _dot(h, wrw_ref[...])
    pbuf[SUBLANES:SUBLANES + tb, :] = p
    p_prev = pbuf[SUBLANES - 1:SUBLANES - 1 + tb, :]
    pbuf[SUBLANES - 1:SUBLANES, :] = p[tb - 1:tb, :]
    pm = p + (p_prev - p) * mu_ref[...]
    r = pm[:, 0:RW_W]
    k = pm[:, RW_W:2 * RW_W]
    v = pm[:, 2 * RW_W:3 * RW_W]
    lora_in = pm[:, 3 * RW_W:3 * RW_W + LANES]
    gd = pm[:, 3 * RW_W + LANES:RW_COLS]
    z = w0_ref[...] + _dot(_bf(jnp.tanh(lora_in)), wdp_ref[...])
    lw_s[...] = (-math.exp(-0.5)) * jax.nn.sigmoid(z)
    a = jax.nn.sigmoid(a0_ref[...] + _dot(_bf(lora_in), wap_ref[...]))
    g_s[...] = _dot(_bf(jax.nn.sigmoid(gd)), gup_ref[...])
    kk = k * kk_ref[...]
    kap = kk * lax.rsqrt(jnp.maximum(segsum(kk * kk), 1e-24))
    k2 = k * (1.0 + (a - 1.0) * ka_ref[...])
    r_s[...] = r
    k_s[...] = k2
    v_s[...] = v
    kap_s[...] = kap
    beta_s[...] = kap * a
    bonus_s[...] = segsum(r * k2 * rk_ref[...]) * v

    assert L == HEAD_DIM
    lane_lo = _iota((L, LANES), 1) < HEAD_DIM
    tok = _iota((L, LANES), 0)
    col = _iota((L, LANES), 1) % L
    strict = col < tok
    incl = col <= tok
    eye_sbs = jnp.where(col == tok, 1.0, 0.0)
    same_head = (_iota((LANES, LANES), 0) // HEAD_DIM) == (_iota((LANES, LANES), 1) // HEAD_DIM)
    ltri = jnp.where(_iota((L, L), 1) <= _iota((L, L), 0), 1.0, 0.0).astype(BF16)

    def stack(zz):
        zero = jnp.zeros_like(zz)
        return jnp.concatenate([jnp.where(lane_lo, zz, zero), jnp.where(lane_lo, zero, zz)], axis=0)

    n_pairs = RW_W // LANES
    n_ch = min(WKV_CHUNKS_PER_STEP, tb // L)

    def phase1(gidx, chains):
        for q in range(n_ch):
            t0 = (gidx * n_ch + q) * L
            rows = slice(t0, t0 + L)
            lwc = lw_s[rows, :]
            l1 = _bf(lwc)
            rem = lwc - l1.astype(F32)
            l2 = _bf(rem)
            l3 = _bf(rem - l2.astype(F32))
            c = _dot(ltri, l1) + _dot(ltri, l2) + _dot(ltri, l3)
            c_last = c[L - 1:L, :]
            e_in = jnp.exp(c)
            e_prev = jnp.exp(c - lwc)
            e_out = jnp.exp(-c)
            e_end = jnp.exp(c_last - c)
            g_end = jnp.exp(c_last)
            kc = k_s[rows, :]
            vc = _bf(v_s[rows, :])
            betac = beta_s[rows, :]
            rt = _bf(r_s[rows, :] * e_in)
            kt = _bf(kc * e_out)
            bt = _bf(betac * e_out)
            kapt = _bf(kap_s[rows, :] * e_prev)
            khat = kc * e_end
            bhat = betac * e_end
            for pr in range(n_pairs):
                sl = slice(LANES * pr, LANES * (pr + 1))
                kts = stack(kt[:, sl])
                bts = stack(bt[:, sl])
                chains.append(dict(
                    q=q, pr=pr, rows=rows, sl=sl, rt=rt[:, sl], kapt=kapt[:, sl], v=vc[:, sl],
                    kb=jnp.concatenate([bts, kts], axis=0), kapts=stack(kapt[:, sl]),
                    vs=stack(vc[:, sl]), khat=khat[:, sl], bhat=bhat[:, sl], g_end=g_end[:, sl]))
            yield
        for ch in chains:
            ma = _dot_nt(jnp.concatenate([ch['kapt'], ch['rt']], axis=0), ch['kb'])
            ch['m_ab'] = jnp.where(strict, ma[:L, :LANES], 0.0)
            ch['m_ak'] = _bf(jnp.where(strict, ma[:L, LANES:], 0.0))
            ch['a_qb'] = _bf(jnp.where(incl, ma[L:, :LANES], 0.0))
            ch['a_qk'] = _bf(jnp.where(incl, ma[L:, LANES:], 0.0))
        yield
        for ch in chains:
            ch['t'] = eye_sbs - ch['m_ab']
            ch['pw'] = _bf(ch['m_ab'])
        for ch in chains:
            ch['pw'] = _bf(_dot(ch['pw'], stack(ch['pw'])))
        yield
        for it in range(5):
            for ch in chains:
                ch['pws'] = stack(ch['pw'])
            if it < 4:
                for ch in chains:
                    both = _dot(jnp.concatenate([ch['pw'], _bf(ch['t'])], axis=0), ch['pws'])
                    ch['pw_next'] = both[:L]
                    ch['t'] = ch['t'] + both[L:]
            else:
                for ch in chains:
                    ch['t'] = ch['t'] + _dot(_bf(ch['t']), ch['pws'])
            if it == 0:
                for ch in chains:
                    mvy = _dot(jnp.concatenate([ch['m_ak'], ch['a_qk']], axis=0), ch['vs'])
                    ch['mvs'] = stack(_bf(mvy[:L]))
                    ch['y0'] = mvy[L:]
            if it == 1:
                for ch in chains:
                    ch['khat_t'] = _bf(ch['khat'].T)
                    ch['bhat_t'] = _bf(ch['bhat'].T)
                    ch['g_rows'] = jnp.broadcast_to(ch['g_end'], (LANES, LANES)).T
            if it == 2:
                for ch in chains:
                    ch['kv0'] = _dot(ch['khat_t'], ch['v'])
            if it < 4:
                for ch in chains:
                    ch['pw'] = _bf(ch['pw_next'])
            yield
        for ch in chains:
            wu = _dot(_bf(ch['t']), jnp.concatenate([ch['kapts'], ch['mvs']], axis=1))
            ch['u0'] = wu[:, LANES:]
            ch['wr'] = jnp.concatenate([_bf(wu[:, :LANES]), ch['rt']], axis=0)
        yield

    def phase2(chains, st):
        for q in range(n_ch):
            cq = [ch for ch in chains if ch['q'] == q]
            st_b = [_bf(s_) for s_ in st]
            ws = [_dot(ch['wr'], st_b[ch['pr']]) for ch in cq]
            u_b = [_bf(ws[ch['pr']][:L] + ch['u0']) for ch in cq]
            yield
            st[:] = [ch['g_rows'] * st[ch['pr']]
                     + jnp.where(same_head, ch['kv0'] - _dot(ch['bhat_t'], u_b[ch['pr']]), 0.0) for ch in cq]
            for ch in cq:
                y_s[ch['rows'], ch['sl']] = (ws[ch['pr']][L:] + ch['y0']
                                             - _dot(ch['a_qb'], stack(u_b[ch['pr']])))
            yield

    def interleave(gens):
        while gens:
            gens = [g_ for g_ in gens if next(g_, 'done') != 'done']

    st = [st_s[pr] for pr in range(n_pairs)]
    prev = None
    for gidx in range(tb // (L * n_ch)):
        cur = []
        interleave([phase1(gidx, cur)] + ([phase2(prev, st)] if prev is not None else []))
        prev = cur
    interleave([phase2(prev, st)])
    for pr in range(n_pairs):
        st_s[pr] = st[pr]

    y = y_s[...]
    mean = segsum(y) * (1.0 / HEAD_DIM)
    d = y - mean
    var = segsum(d * d) * (1.0 / HEAD_DIM)
    yn = d * lax.rsqrt(var + LNX_EPS) * lng_ref[...] + lnb_ref[...]
    o_ref[0] = _bf((yn + bonus_s[...]) * g_s[...])


def _rwkv_call(x, g1, wrw, mu, wdp, wap, gup, w0, a0, kk, ka, rk, lng, lnb, ts):
    b, s, _ = x.shape
    const = lambda shape: pl.BlockSpec(shape, lambda i, j: (0,) * len(shape))
    row = const((1, RW_W))
    scr = lambda: pltpu.VMEM((ts, RW_W), F32)
    return pl.pallas_call(
        _rwkv_kernel,
        out_shape=jax.ShapeDtypeStruct((b, s, RW_W), BF16),
        grid=(b, s // ts),
        in_specs=[pl.BlockSpec((1, ts, D_MODEL), lambda i, j: (i, j, 0)),
                  const((1, D_MODEL)), const((D_MODEL, RW_COLS)), const((1, RW_COLS)),
                  const((LANES, RW_W)), const((LANES, RW_W)), const((GATE_LORA, RW_W)),
                  row, row, row, row, row, row, row],
        out_specs=pl.BlockSpec((1, ts, RW_W), lambda i, j: (i, j, 0)),
        scratch_shapes=[pltpu.VMEM((ts + SUBLANES, RW_COLS), F32),
                        scr(), scr(), scr(), scr(), scr(), scr(), scr(), scr(), scr(),
                        pltpu.VMEM((RW_W // LANES, LANES, LANES), F32)],
        compiler_params=pltpu.CompilerParams(dimension_semantics=("arbitrary", "arbitrary"),
                                             vmem_limit_bytes=VMEM_LIMIT),
        name="rwkv_mixer",
    )(x, g1, wrw, mu, wdp, wap, gup, w0, a0, kk, ka, rk, lng, lnb)


def _poolconv_init(ubuf, cbuf):
    ubuf[0:POOL_MAXW, :] = jnp.zeros((POOL_MAXW, POOL_W), F32)
    cbuf[0:SUBLANES, :] = jnp.zeros((SUBLANES, CONV_W), F32)


def _poolconv_stages(j, h, wpc_ref, poolw_ref, pscale_ref, convw_ref, ob_ref, oc_ref, ubuf, cbuf):
    tb = h.shape[0]
    halo = POOL_MAXW
    u = _dot(h, wpc_ref[:, 0:POOL_W])
    ubuf[halo:halo + tb, :] = u
    yield
    t_glob = j * tb + _iota((tb, 1), 0)
    for gi, win in enumerate(POOL_WINDOWS):
        cs = slice(POOL_GW * gi, POOL_GW * (gi + 1))
        acc = u[:, cs]
        for dlt in range(1, win):
            acc = acc + ubuf[halo - dlt:halo - dlt + tb, cs]
        cnt = jnp.minimum(t_glob + 1, win).astype(F32)
        zc = acc / cnt - u[:, cs]
        zz = _dot(_bf(zc), poolw_ref[gi])
        ob_ref[0, :, cs] = _bf(zz * pscale_ref[:, cs])
        yield
    ubuf[0:halo, :] = ubuf[tb:tb + halo, :]

    cg = _dot(h, wpc_ref[:, POOL_W + CONV_W:POOL_W + 2 * CONV_W])
    yield
    cu = _dot(h, wpc_ref[:, POOL_W + 2 * CONV_W:POOL_W + 3 * CONV_W])
    vv = cg * cu
    cbuf[SUBLANES:SUBLANES + tb, :] = vv
    yield
    bg = _dot(h, wpc_ref[:, POOL_W:POOL_W + CONV_W])
    conv = (convw_ref[0:1, :] * cbuf[SUBLANES - 2:SUBLANES - 2 + tb, :]
            + convw_ref[1:2, :] * cbuf[SUBLANES - 1:SUBLANES - 1 + tb, :]
            + convw_ref[2:3, :] * vv)
    cbuf[0:SUBLANES, :] = cbuf[tb:tb + SUBLANES, :]
    oc_ref[0] = _bf(bg * conv)
    yield


def _attn_init(sink_ref, kbuf0, kbuf1, vbuf, bias_tab):
    blk = ATT_BLOCK
    rows = ATT_G * blk
    kbuf0[0:blk, :] = jnp.zeros((blk, LANES), BF16)
    kbuf1[0:blk, :] = jnp.zeros((blk, LANES), BF16)
    vbuf[0:blk, :] = jnp.zeros((blk, LANES), BF16)
    qi = _iota((rows, 2 * blk), 0) % blk + blk
    kj = _iota((rows, 2 * blk), 1)
    dist = qi - kj
    band = (dist >= 0) & (dist < ATT_BLOCK)
    for g in range(ATT_HKV):
        sink = sink_ref[g]
        bias_tab[0, g] = jnp.where(kj == 0, sink, jnp.where(band & (kj >= blk), 0.0, -jnp.inf))
        bias_tab[1, g] = jnp.where(kj == 0, sink, jnp.where(band, 0.0, -jnp.inf))


def _attn_stages(j, h, watt_ref, qg_ref, kg_ref, cos_ref, sin_ref, o_ref, kbuf0, kbuf1, vbuf, bias_tab):
    tq = h.shape[0]
    blk = ATT_BLOCK
    nq = ATT_HQ * ATT_HD // LANES
    rows = nq * blk

    hi_ = (_iota((LANES, LANES), 0) % ATT_HD) // (ATT_HD // 2)
    hj_ = (_iota((LANES, LANES), 1) % ATT_HD) // (ATT_HD // 2)
    seg_mean = jnp.where(hi_ == hj_, 1.0 / ATT_HD, 0.0).astype(BF16)

    p = _dot(h, watt_ref[...])
    yield
    cos = cos_ref[0]
    sin = sin_ref[0]

    def norm_rope(xb, gain):
        ms = _dot_x2(xb * xb, seg_mean)
        yb = xb * lax.rsqrt(ms + NORM_EPS) * gain
        return yb * cos + pltpu.roll(yb, LANES // 2, axis=1) * sin

    kn = norm_rope(p[:, nq * LANES:(nq + 1) * LANES], kg_ref[...])
    kv_lane = (_iota((tq, LANES), 1) % ATT_HD) // (ATT_HD // 2)
    kbuf0[blk:blk + tq, :] = _bf(jnp.where(kv_lane == 0, kn, 0.0))
    kbuf1[blk:blk + tq, :] = _bf(jnp.where(kv_lane == 1, kn, 0.0))
    vbuf[blk:blk + tq, :] = _bf(p[:, (nq + 1) * LANES:(nq + 2) * LANES])
    yield
    qs = [_bf(norm_rope(p[:, LANES * b:LANES * (b + 1)], qg_ref[...]) * (ATT_HD ** -0.5))
          for b in range(nq)]
    yield

    out_lo = _iota((rows, LANES), 1) < ATT_HD
    not_sink_row = _iota((2 * blk, LANES), 0) > 0
    ones_cols = jnp.ones((2 * blk, LANES), BF16)
    zero_kv = jnp.zeros((2 * blk, LANES), BF16)
    first_tab = jnp.where(j == 0, 0, 1)
    nblk = tq // blk
    nstep = min(ATT_BLOCKS_PER_STEP, nblk)
    for n0 in range(0, nblk, nstep):
        chains = []
        for n in range(n0, n0 + nstep):
            q_st = jnp.concatenate([q[blk * n:blk * (n + 1), :] for q in qs], axis=0)
            win = slice(blk * n, blk * (n + 2))
            v1 = jnp.concatenate([jnp.where(not_sink_row, vbuf[win, :], zero_kv), ones_cols], axis=1)
            for g, kbuf in enumerate((kbuf0, kbuf1)):
                chains.append(dict(n=n, g=g, q=q_st, v1=v1,
                                   kb=jnp.where(not_sink_row, kbuf[win, :], zero_kv)))
        for ch in chains:
            bias = bias_tab[first_tab, ch['g']] if ch['n'] == 0 else bias_tab[1, ch['g']]
            ch['sc'] = _dot_nt(ch['q'], ch['kb']) + bias
        yield
        for ch in chains:
            ch['m'] = jnp.max(ch['sc'], axis=-1, keepdims=True)
        yield
        for ch in chains:
            ch['e'] = _bf(jnp.exp(ch['sc'] - ch['m']))
        yield
        for ch in chains:
            ch['o'] = _dot(ch['e'], ch['v1'])
        yield
        for n in range(n0, n0 + nstep):
            c0, c1 = [ch for ch in chains if ch['n'] == n]
            num = jnp.where(out_lo, c0['o'][:, :LANES], c1['o'][:, :LANES])
            den = jnp.where(out_lo, c0['o'][:, LANES:], c1['o'][:, LANES:])
            o = num / den
            for b in range(nq):
                o_ref[0, blk * n:blk * (n + 1), LANES * b:LANES * (b + 1)] = _bf(o[blk * b:blk * (b + 1), :])
    for buf in (kbuf0, kbuf1, vbuf):
        buf[0:blk, :] = buf[tq:tq + blk, :]


def _interleave(gens):
    while gens:
        gens = [g_ for g_ in gens if next(g_, 'done') != 'done']


def _mixers_kernel(x_ref, g1_ref, watt_ref, qg_ref, kg_ref, cos_ref, sin_ref, sink_ref,
                   wpc_ref, poolw_ref, pscale_ref, convw_ref, od_ref, ob_ref, oc_ref,
                   kbuf0, kbuf1, vbuf, bias_tab, ubuf, cbuf):
    j = pl.program_id(1)

    @pl.when(j == 0)
    def _():
        _attn_init(sink_ref, kbuf0, kbuf1, vbuf, bias_tab)
        _poolconv_init(ubuf, cbuf)

    h = _bf(_rmsnorm(x_ref[0], g1_ref[...]))
    _interleave([
        _attn_stages(j, h, watt_ref, qg_ref, kg_ref, cos_ref, sin_ref, od_ref, kbuf0, kbuf1, vbuf, bias_tab),
        _poolconv_stages(j, h, wpc_ref, poolw_ref, pscale_ref, convw_ref, ob_ref, oc_ref, ubuf, cbuf)])


def _mixers_call(x, g1, watt, qg, kg, cos, sin, sink_rows, wpc, poolw, pscale, convw, ts):
    b, s, _ = x.shape
    const = lambda shape: pl.BlockSpec(shape, lambda i, j: (0,) * len(shape))
    tile = lambda w: pl.BlockSpec((1, ts, w), lambda i, j: (i, j, 0))
    out = lambda w: jax.ShapeDtypeStruct((b, s, w), BF16)
    return pl.pallas_call(
        _mixers_kernel,
        out_shape=(out(ATT_HQ * ATT_HD), out(POOL_W), out(CONV_W)),
        grid=(b, s // ts),
        in_specs=[tile(D_MODEL), const((1, D_MODEL)), const((D_MODEL, ATT_COLS)),
                  const((1, LANES)), const((1, LANES)), tile(LANES), tile(LANES),
                  const((ATT_HKV, ATT_G * ATT_BLOCK, 1)),
                  const((D_MODEL, POOL_W + 3 * CONV_W)),
                  const((len(POOL_WINDOWS), POOL_GW, POOL_GW)), const((1, POOL_W)),
                  const((CONV_K, CONV_W))],
        out_specs=(tile(ATT_HQ * ATT_HD), tile(POOL_W), tile(CONV_W)),
        scratch_shapes=[pltpu.VMEM((ts + ATT_BLOCK, LANES), BF16),
                        pltpu.VMEM((ts + ATT_BLOCK, LANES), BF16),
                        pltpu.VMEM((ts + ATT_BLOCK, LANES), BF16),
                        pltpu.VMEM((2, ATT_HKV, ATT_G * ATT_BLOCK, 2 * ATT_BLOCK), F32),
                        pltpu.VMEM((ts + POOL_MAXW, POOL_W), F32),
                        pltpu.VMEM((ts + SUBLANES, CONV_W), F32)],
        compiler_params=pltpu.CompilerParams(dimension_semantics=("arbitrary", "arbitrary"),
                                             vmem_limit_bytes=VMEM_LIMIT),
        name="attn_pool_conv_mixers",
    )(x, g1, watt, qg, kg, cos, sin, sink_rows, wpc, poolw, pscale, convw)


def _merge_kernel(x_ref, g1_ref, wg_ref, za_ref, zb_ref, zc_ref, zd_ref,
                  wa_ref, wb_ref, wc_ref, wd_ref, wo_ref, o_ref):
    x = x_ref[...]
    h = _bf(_rmsnorm(x, g1_ref[...]))
    mixed = None
    for b, (z_ref, w_ref) in enumerate(((za_ref, wa_ref), (zb_ref, wb_ref), (zc_ref, wc_ref), (zd_ref, wd_ref))):
        gate = jax.nn.sigmoid(_dot(h, wg_ref[:, D_MODEL * b:D_MODEL * (b + 1)]))
        term = gate * _dot(z_ref[...], w_ref[...])
        mixed = term if mixed is None else mixed + term
    o_ref[...] = x + _dot(_bf(mixed), wo_ref[...])


def _merge_call(x2, g1, wg, za, zb, zc, zd, wa, wb, wc, wd, wo, tm):
    t = x2.shape[0]
    const = lambda shape: pl.BlockSpec(shape, lambda i: (0,) * len(shape))
    tile = lambda w: pl.BlockSpec((tm, w), lambda i: (i, 0))
    wout = const((RW_W, D_MODEL))
    return pl.pallas_call(
        _merge_kernel,
        out_shape=jax.ShapeDtypeStruct((t, D_MODEL), F32),
        grid=(t // tm,),
        in_specs=[tile(D_MODEL), const((1, D_MODEL)), const((D_MODEL, N_BRANCH * D_MODEL)),
                  tile(RW_W), tile(POOL_W), tile(CONV_W), tile(ATT_HQ * ATT_HD),
                  wout, wout, wout, wout, const((D_MODEL, D_MODEL))],
        out_specs=tile(D_MODEL),
        compiler_params=pltpu.CompilerParams(dimension_semantics=("arbitrary",),
                                             vmem_limit_bytes=VMEM_LIMIT),
        name="merge_mixers",
    )(x2, g1, wg, za, zb, zc, zd, wa, wb, wc, wd, wo)


def _ffn_kernel(x_ref, g2_ref, wg_ref, wu_ref, wd_ref, o_ref):
    x = x_ref[...]
    h = _bf(_rmsnorm(x, g2_ref[...]))
    acc = x
    for c in range(D_FF // FFN_CHUNK):
        cs = slice(FFN_CHUNK * c, FFN_CHUNK * (c + 1))
        gt = _dot(h, wg_ref[:, cs])
        up = _dot(h, wu_ref[:, cs])
        act = _bf(gt * jax.nn.sigmoid(gt) * up)
        acc = acc + _dot(act, wd_ref[cs, :])
    o_ref[...] = acc


def _ffn_call(x2, g2, wg, wu, wd, tm):
    t = x2.shape[0]
    const = lambda shape: pl.BlockSpec(shape, lambda i: (0,) * len(shape))
    tile = pl.BlockSpec((tm, D_MODEL), lambda i: (i, 0))
    return pl.pallas_call(
        _ffn_kernel,
        out_shape=jax.ShapeDtypeStruct((t, D_MODEL), F32),
        grid=(t // tm,),
        in_specs=[tile, const((1, D_MODEL)), const((D_MODEL, D_FF)), const((D_MODEL, D_FF)),
                  const((D_FF, D_MODEL))],
        out_specs=tile,
        compiler_params=pltpu.CompilerParams(dimension_semantics=("arbitrary",),
                                             vmem_limit_bytes=VMEM_LIMIT),
        name="ffn_swiglu",
    )(x2, g2, wg, wu, wd)


def _attn_perms():
    half = ATT_HD // 2
    q_cols = []
    for jb in range(ATT_G):
        for hf in range(2):
            for ab in range(ATT_HKV):
                head = ATT_G * ab + jb
                q_cols += [ATT_HD * head + half * hf + i for i in range(half)]
    k_cols = []
    for hf in range(2):
        for g in range(ATT_HKV):
            k_cols += [ATT_HD * g + half * hf + i for i in range(half)]
    gain_idx = [half * ((l % LANES) // (LANES // 2)) + l % half for l in range(LANES)]
    o_rows = []
    for jb in range(ATT_G):
        for ab in range(ATT_HKV):
            head = ATT_G * ab + jb
            o_rows += [ATT_HD * head + c for c in range(ATT_HD)]
    return q_cols, k_cols, gain_idx, o_rows


def kernel(x, positions, norm1_g, w_in, shift_mu, w_decay_up, w0, a_up, a0, g_up, k_k, k_a, r_k, lnx_g, lnx_b, w_rwkv_out, pool_w, pool_scale, w_pool_out, conv_w, w_conv_out, q_norm_g, k_norm_g, sinks, w_attn_out, w_o, norm2_g, w_ffn_gate, w_ffn_up, w_ffn_down):
    b, s, d = x.shape
    assert d == D_MODEL and s % ATT_BLOCK == 0
    ts = min(SEQ_TILE, s)
    tm = min(TOK_TILE, b * s)
    assert s % ts == 0 and (b * s) % tm == 0
    depth = w_in.shape[0]

    q_cols, k_cols, gain_idx, o_rows = _attn_perms()
    q_cols = jnp.asarray(q_cols, jnp.int32)
    k_cols = jnp.asarray(k_cols, jnp.int32)
    gain_idx = jnp.asarray(gain_idx, jnp.int32)
    o_rows = jnp.asarray(o_rows, jnp.int32)
    c_rw, c_pool, c_conv, c_att = RW_COLS, RW_COLS + POOL_W, RW_COLS + POOL_W + 3 * CONV_W, \
        RW_COLS + POOL_W + 3 * CONV_W + ATT_COLS

    cos, sin = _rope_tables(positions, ts)
    row = lambda v: v.reshape(1, -1).astype(F32)
    zeros_lora = jnp.zeros((DECAY_LORA, RW_W), F32)

    for i in range(depth):
        g1 = row(norm1_g[i])
        wi = w_in[i]
        wdp = _bf(jnp.concatenate([w_decay_up[i], zeros_lora], axis=0))
        wap = _bf(jnp.concatenate([zeros_lora, a_up[i]], axis=0))
        za = _rwkv_call(x, g1, _bf(wi[:, :c_rw]), row(shift_mu[i]), wdp, wap, _bf(g_up[i]),
                        row(w0[i]), row(a0[i]), row(k_k[i]), row(k_a[i]), row(r_k[i]),
                        row(lnx_g[i]), row(lnx_b[i]), ts)
        w_att = wi[:, c_conv:c_att]
        w_att = jnp.concatenate([w_att[:, q_cols], w_att[:, ATT_HQ * ATT_HD + k_cols],
                                 w_att[:, (ATT_HQ + ATT_HKV) * ATT_HD:]], axis=1)
        sink_rows = jnp.repeat(sinks[i].astype(F32).reshape(ATT_HKV, ATT_G), ATT_BLOCK, axis=1)[..., None]
        zd, zb, zc = _mixers_call(x, g1, _bf(w_att), row(q_norm_g[i][gain_idx]), row(k_norm_g[i][gain_idx]),
                                  cos, sin, sink_rows, _bf(wi[:, c_rw:c_conv]), _bf(pool_w[i]),
                                  row(pool_scale[i]), conv_w[i].astype(F32), ts)
        flat = lambda z: z.reshape(b * s, z.shape[-1])
        x1 = _merge_call(flat(x), g1, _bf(wi[:, c_att:]), flat(za), flat(zb), flat(zc), flat(zd),
                         _bf(w_rwkv_out[i]), _bf(w_pool_out[i]), _bf(w_conv_out[i]),
                         _bf(w_attn_out[i][o_rows, :]), _bf(w_o[i]), tm)
        x2 = _ffn_call(x1, row(norm2_g[i]), _bf(w_ffn_gate[i]), _bf(w_ffn_up[i]), _bf(w_ffn_down[i]), tm)
        x = x2.reshape(b, s, d)
    return x
```

```python
import functools
import math

import jax
import jax.numpy as jnp
from jax import lax
from jax.experimental import pallas as pl
from jax.experimental.pallas import tpu as pltpu

F32 = jnp.float32
BF16 = jnp.bfloat16

D_MODEL = 1024
RW_HEADS = 8
HEAD_DIM = 64
RW_W = RW_HEADS * HEAD_DIM
DECAY_LORA = 64
ICLR_LORA = 64
GATE_LORA = 128
LNX_EPS = 64e-5
RW_COLS = 3 * RW_W + DECAY_LORA + ICLR_LORA + GATE_LORA
POOL_W = 512
POOL_GW = 128
POOL_WINDOWS = (2, 4, 8, 16)
POOL_MAXW = 16
CONV_W = 512
CONV_K = 3
ATT_HQ = 8
ATT_HKV = 2
ATT_G = ATT_HQ // ATT_HKV
ATT_HD = 64
ATT_BLOCK = 128
ATT_COLS = (ATT_HQ + 2 * ATT_HKV) * ATT_HD
ROPE_THETA = 10000.0
N_BRANCH = 4
D_FF = 2816
NORM_EPS = 1e-6

LANES = 128
SUBLANES = 8
WKV_CHUNK = 64
WKV_CHUNKS_PER_STEP = 4
ATT_BLOCKS_PER_STEP = 2
SEQ_TILE = 512
TOK_TILE = 1024
FFN_CHUNK = 256
VMEM_LIMIT = 48 * 1024 * 1024


def _bf(x):
    return x.astype(BF16)


def _dot(a, b):
    return jnp.dot(a, b, preferred_element_type=F32)


def _dot_nt(a, b):
    return lax.dot_general(a, b, (((1,), (1,)), ((), ())), preferred_element_type=F32)


def _split2(x):
    hi = _bf(x)
    lo = _bf(x - hi.astype(F32))
    return hi, lo


def _dot_x2(x, m):
    hi, lo = _split2(x)
    return _dot(hi, m) + _dot(lo, m)


def _rmsnorm(x, g):
    ms = jnp.mean(x * x, axis=-1, keepdims=True)
    return x * lax.rsqrt(ms + NORM_EPS) * g


def _iota(shape, dim):
    return lax.broadcasted_iota(jnp.int32, shape, dim)


def _rope_kernel(pos_ref, inv_ref, cos_ref, sin_ref):
    ang = pos_ref[0].astype(F32) * inv_ref[...]
    lane = _iota(ang.shape, 1)
    cos_ref[0] = jnp.cos(ang)
    s = jnp.sin(ang)
    sin_ref[0] = jnp.where(lane < LANES // 2, -s, s)


def _rope_tables(positions, ts):
    b, s = positions.shape
    half = ATT_HD // 2
    inv = ROPE_THETA ** (-jnp.arange(half, dtype=F32) * 2.0 / ATT_HD)
    inv = jnp.tile(inv, LANES // half)[None, :]
    pos3 = positions.reshape(b, s, 1)
    out = jax.ShapeDtypeStruct((b, s, LANES), F32)
    return pl.pallas_call(
        _rope_kernel,
        out_shape=(out, out),
        grid=(b, s // ts),
        in_specs=[pl.BlockSpec((1, ts, 1), lambda i, j: (i, j, 0)),
                  pl.BlockSpec((1, LANES), lambda i, j: (0, 0))],
        out_specs=(pl.BlockSpec((1, ts, LANES), lambda i, j: (i, j, 0)),
                   pl.BlockSpec((1, ts, LANES), lambda i, j: (i, j, 0))),
        compiler_params=pltpu.CompilerParams(dimension_semantics=("arbitrary", "arbitrary")),
        name="rope_tables",
    )(pos3, inv)


def _rwkv_init(pbuf, st_s):
    st_s[...] = jnp.zeros(st_s.shape, F32)
    pbuf[0:SUBLANES, :] = jnp.zeros((SUBLANES, RW_COLS), F32)


def _rwkv_body(h, wrw_ref, mu_ref, wdp_ref, wap_ref, gup_ref, w0_ref, a0_ref,
               kk_ref, ka_ref, rk_ref, lng_ref, lnb_ref, o_ref,
               pbuf, r_s, k_s, v_s, kap_s, beta_s, lw_s, bonus_s, g_s, y_s, st_s, background):
    tb = h.shape[0]
    L = WKV_CHUNK

    gi = _iota((LANES, LANES), 0) // HEAD_DIM
    gj = _iota((LANES, LANES), 1) // HEAD_DIM
    seg_ones = jnp.where(gi == gj, 1.0, 0.0).astype(BF16)

    def segsum(z):
        return jnp.concatenate(
            [_dot(_bf(z[:, LANES * b:LANES * (b + 1)]), seg_ones) for b in range(RW_W // LANES)], axis=1)

    p = _dot(h, wrw_ref[...])
    pbuf[SUBLANES:SUBLANES + tb, :] = p
    p_prev = pbuf[SUBLANES - 1:SUBLANES - 1 + tb, :]
    pbuf[SUBLANES - 1:SUBLANES, :] = p[tb - 1:tb, :]
    pm = p + (p_prev - p) * mu_ref[...]
    r = pm[:, 0:RW_W]
    k = pm[:, RW_W:2 * RW_W]
    v = pm[:, 2 * RW_W:3 * RW_W]
    lora_in = pm[:, 3 * RW_W:3 * RW_W + LANES]
    gd = pm[:, 3 * RW_W + LANES:RW_COLS]
    z = w0_ref[...] + _dot(_bf(jnp.tanh(lora_in)), wdp_ref[...])
    lw_s[...] = (-math.exp(-0.5)) * jax.nn.sigmoid(z)
    a = jax.nn.sigmoid(a0_ref[...] + _dot(_bf(lora_in), wap_ref[...]))
    g_s[...] = _dot(_bf(jax.nn.sigmoid(gd)), gup_ref[...])
    kk = k * kk_ref[...]
    kap = kk * lax.rsqrt(jnp.maximum(segsum(kk * kk), 1e-24))
    k2 = k * (1.0 + (a - 1.0) * ka_ref[...])
    r_s[...] = r
    k_s[...] = k2
    v_s[...] = v
    kap_s[...] = kap
    beta_s[...] = kap * a
    bonus_s[...] = segsum(r * k2 * rk_ref[...]) * v

    assert L == HEAD_DIM
    lane_lo = _iota((L, LANES), 1) < HEAD_DIM
    tok = _iota((L, LANES), 0)
    col = _iota((L, LANES), 1) % L
    strict = col < tok
    incl = col <= tok
    eye_sbs = jnp.where(col == tok, 1.0, 0.0)
    same_head = (_iota((LANES, LANES), 0) // HEAD_DIM) == (_iota((LANES, LANES), 1) // HEAD_DIM)
    ltri = jnp.where(_iota((L, L), 1) <= _iota((L, L), 0), 1.0, 0.0).astype(BF16)

    def stack(zz):
        zero = jnp.zeros_like(zz)
        return jnp.concatenate([jnp.where(lane_lo, zz, zero), jnp.where(lane_lo, zero, zz)], axis=0)

    n_pairs = RW_W // LANES
    n_ch = min(WKV_CHUNKS_PER_STEP, tb // L)

    def phase1(gidx, chains):
        for q in range(n_ch):
            t0 = (gidx * n_ch + q) * L
            rows = slice(t0, t0 + L)
            lwc = lw_s[rows, :]
            l1, l2 = _split2(lwc)
            c = _dot(ltri, l1) + _dot(ltri, l2)
            c_last = c[L - 1:L, :]
            e_in = jnp.exp(c)
            e_prev = jnp.exp(c - lwc)
            e_out = jnp.exp(-c)
            g_end = jnp.exp(c_last)
            e_end = g_end * e_out
            kc = k_s[rows, :]
            vc = _bf(v_s[rows, :])
            betac = beta_s[rows, :]
            rt = _bf(r_s[rows, :] * e_in)
            kt = _bf(kc * e_out)
            bt = _bf(betac * e_out)
            kapt = _bf(kap_s[rows, :] * e_prev)
            khat = kc * e_end
            bhat = betac * e_end
            for pr in range(n_pairs):
                sl = slice(LANES * pr, LANES * (pr + 1))
                kts = stack(kt[:, sl])
                bts = stack(bt[:, sl])
                chains.append(dict(
                    q=q, pr=pr, rows=rows, sl=sl, rt=rt[:, sl], kapt=kapt[:, sl], v=vc[:, sl],
                    kb=jnp.concatenate([bts, kts], axis=0), kapts=stack(kapt[:, sl]),
                    vs=stack(vc[:, sl]), khat=khat[:, sl], bhat=bhat[:, sl], g_end=g_end[:, sl]))
            yield
        for ch in chains:
            ma = _dot_nt(jnp.concatenate([ch['kapt'], ch['rt']], axis=0), ch['kb'])
            ch['m_ab'] = jnp.where(strict, ma[:L, :LANES], 0.0)
            ch['m_ak'] = _bf(jnp.where(strict, ma[:L, LANES:], 0.0))
            ch['a_qb'] = _bf(jnp.where(incl, ma[L:, :LANES], 0.0))
            ch['a_qk'] = _bf(jnp.where(incl, ma[L:, LANES:], 0.0))
        yield
        for ch in chains:
            ch['t'] = eye_sbs - ch['m_ab']
            ch['pw'] = _bf(ch['m_ab'])
        for ch in chains:
            ch['pw'] = _bf(_dot(ch['pw'], stack(ch['pw'])))
        yield
        for it in range(5):
            for ch in chains:
                ch['pws'] = stack(ch['pw'])
            if it < 4:
                for ch in chains:
                    both = _dot(jnp.concatenate([ch['pw'], _bf(ch['t'])], axis=0), ch['pws'])
                    ch['pw_next'] = both[:L]
                    ch['t'] = ch['t'] + both[L:]
            else:
                for ch in chains:
                    ch['t'] = ch['t'] + _dot(_bf(ch['t']), ch['pws'])
            if it == 0:
                for ch in chains:
                    mvy = _dot(jnp.concatenate([ch['m_ak'], ch['a_qk']], axis=0), ch['vs'])
                    ch['mvs'] = stack(_bf(mvy[:L]))
                    ch['y0'] = mvy[L:]
            if it == 1:
                for ch in chains:
                    ch['khat_t'] = _bf(ch['khat'].T)
                    ch['bhat_t'] = _bf(ch['bhat'].T)
                    ch['g_rows'] = jnp.broadcast_to(ch['g_end'], (LANES, LANES)).T
            if it == 2:
                for ch in chains:
                    ch['kv0'] = _dot(ch['khat_t'], ch['v'])
            if it < 4:
                for ch in chains:
                    ch['pw'] = _bf(ch['pw_next'])
            yield
        for ch in chains:
            wu = _dot(_bf(ch['t']), jnp.concatenate([ch['kapts'], ch['mvs']], axis=1))
            ch['u0'] = wu[:, LANES:]
            ch['wr'] = jnp.concatenate([_bf(wu[:, :LANES]), ch['rt']], axis=0)
        yield

    def phase2(chains, st):
        for q in range(n_ch):
            cq = [ch for ch in chains if ch['q'] == q]
            st_b = [_bf(s_) for s_ in st]
            ws = [_dot(ch['wr'], st_b[ch['pr']]) for ch in cq]
            u_b = [_bf(ws[ch['pr']][:L] + ch['u0']) for ch in cq]
            yield
            st[:] = [ch['g_rows'] * st[ch['pr']]
                     + jnp.where(same_head, ch['kv0'] - _dot(ch['bhat_t'], u_b[ch['pr']]), 0.0) for ch in cq]
            for ch in cq:
                y_s[ch['rows'], ch['sl']] = (ws[ch['pr']][L:] + ch['y0']
                                             - _dot(ch['a_qb'], stack(u_b[ch['pr']])))
            yield

    def stage_c(gidx):
        rows = slice(gidx * n_ch * L, (gidx + 1) * n_ch * L)
        y = y_s[rows, :]
        mean = segsum(y) * (1.0 / HEAD_DIM)
        yield
        d = y - mean
        var = segsum(d * d) * (1.0 / HEAD_DIM)
        yield
        yn = d * lax.rsqrt(var + LNX_EPS) * lng_ref[...] + lnb_ref[...]
        o_ref[0, rows, :] = _bf((yn + bonus_s[rows, :]) * g_s[rows, :])
        yield

    st = [st_s[pr] for pr in range(n_pairs)]
    chains_of = {}
    n_groups = tb // (L * n_ch)
    for step in range(n_groups + 2):
        gens = []
        if step < n_groups:
            chains_of[step] = []
            gens.append(phase1(step, chains_of[step]))
        if 0 <= step - 1 < n_groups:
            gens.append(phase2(chains_of.pop(step - 1), st))
        if 0 <= step - 2 < n_groups:
            gens.append(stage_c(step - 2))
        _interleave(gens, background.get(step, ()))
    for pr in range(n_pairs):
        st_s[pr] = st[pr]
    _interleave([g_ for gs_ in background.values() for g_ in gs_])


def _poolconv_init(ubuf, cbuf):
    ubuf[0:POOL_MAXW, :] = jnp.zeros((POOL_MAXW, POOL_W), F32)
    cbuf[0:SUBLANES, :] = jnp.zeros((SUBLANES, CONV_W), F32)


def _poolconv_stages(j, h, wpc_ref, poolw_ref, pscale_ref, convw_ref, ob_ref, oc_ref, ubuf, cbuf):
    tb = h.shape[0]
    halo = POOL_MAXW
    u = _dot(h, wpc_ref[:, 0:POOL_W])
    ubuf[halo:halo + tb, :] = u
    yield
    t_glob = j * tb + _iota((tb, 1), 0)
    for gi, win in enumerate(POOL_WINDOWS):
        cs = slice(POOL_GW * gi, POOL_GW * (gi + 1))
        acc = u[:, cs]
        for dlt in range(1, win):
            acc = acc + ubuf[halo - dlt:halo - dlt + tb, cs]
        cnt = jnp.minimum(t_glob + 1, win).astype(F32)
        zc = acc / cnt - u[:, cs]
        zz = _dot(_bf(zc), poolw_ref[gi])
        ob_ref[0, :, cs] = _bf(zz * pscale_ref[:, cs])
        yield
    ubuf[0:halo, :] = ubuf[tb:tb + halo, :]

    cg = _dot(h, wpc_ref[:, POOL_W + CONV_W:POOL_W + 2 * CONV_W])
    yield
    cu = _dot(h, wpc_ref[:, POOL_W + 2 * CONV_W:POOL_W + 3 * CONV_W])
    vv = cg * cu
    cbuf[SUBLANES:SUBLANES + tb, :] = vv
    yield
    bg = _dot(h, wpc_ref[:, POOL_W:POOL_W + CONV_W])
    conv = (convw_ref[0:1, :] * cbuf[SUBLANES - 2:SUBLANES - 2 + tb, :]
            + convw_ref[1:2, :] * cbuf[SUBLANES - 1:SUBLANES - 1 + tb, :]
            + convw_ref[2:3, :] * vv)
    cbuf[0:SUBLANES, :] = cbuf[tb:tb + SUBLANES, :]
    oc_ref[0] = _bf(bg * conv)
    yield


def _attn_init(sink_ref, kbuf0, kbuf1, vbuf, bias_tab):
    blk = ATT_BLOCK
    rows = ATT_G * blk
    kbuf0[0:blk, :] = jnp.zeros((blk, LANES), BF16)
    kbuf1[0:blk, :] = jnp.zeros((blk, LANES), BF16)
    vbuf[0:blk, :] = jnp.zeros((blk, LANES), BF16)
    qi = _iota((rows, 2 * blk), 0) % blk + blk
    kj = _iota((rows, 2 * blk), 1)
    dist = qi - kj
    band = (dist >= 0) & (dist < ATT_BLOCK)
    for g in range(ATT_HKV):
        sink = sink_ref[g]
        bias_tab[0, g] = jnp.where(kj == 0, sink, jnp.where(band & (kj >= blk), 0.0, -jnp.inf))
        bias_tab[1, g] = jnp.where(kj == 0, sink, jnp.where(band, 0.0, -jnp.inf))


def _attn_stages(j, h, watt_ref, qg_ref, kg_ref, cos_ref, sin_ref, o_ref, kbuf0, kbuf1, vbuf, bias_tab):
    tq = h.shape[0]
    blk = ATT_BLOCK
    nq = ATT_HQ * ATT_HD // LANES
    rows = nq * blk

    hi_ = (_iota((LANES, LANES), 0) % ATT_HD) // (ATT_HD // 2)
    hj_ = (_iota((LANES, LANES), 1) % ATT_HD) // (ATT_HD // 2)
    seg_mean = jnp.where(hi_ == hj_, 1.0 / ATT_HD, 0.0).astype(BF16)

    p = _dot(h, watt_ref[...])
    yield
    cos = cos_ref[0]
    sin = sin_ref[0]

    def norm_rope(xb, gain):
        ms = _dot_x2(xb * xb, seg_mean)
        yb = xb * lax.rsqrt(ms + NORM_EPS) * gain
        return yb * cos + pltpu.roll(yb, LANES // 2, axis=1) * sin

    kn = norm_rope(p[:, nq * LANES:(nq + 1) * LANES], kg_ref[...])
    kv_lane = (_iota((tq, LANES), 1) % ATT_HD) // (ATT_HD // 2)
    kbuf0[blk:blk + tq, :] = _bf(jnp.where(kv_lane == 0, kn, 0.0))
    kbuf1[blk:blk + tq, :] = _bf(jnp.where(kv_lane == 1, kn, 0.0))
    vbuf[blk:blk + tq, :] = _bf(p[:, (nq + 1) * LANES:(nq + 2) * LANES])
    yield
    qs = [_bf(norm_rope(p[:, LANES * b:LANES * (b + 1)], qg_ref[...]) * (ATT_HD ** -0.5))
          for b in range(nq)]
    yield

    out_lo = _iota((rows, LANES), 1) < ATT_HD
    not_sink_row = _iota((2 * blk, LANES), 0) > 0
    ones_cols = jnp.ones((2 * blk, LANES), BF16)
    zero_kv = jnp.zeros((2 * blk, LANES), BF16)
    first_tab = jnp.where(j == 0, 0, 1)
    nblk = tq // blk
    nstep = min(ATT_BLOCKS_PER_STEP, nblk)
    for n0 in range(0, nblk, nstep):
        chains = []
        for n in range(n0, n0 + nstep):
            q_st = jnp.concatenate([q[blk * n:blk * (n + 1), :] for q in qs], axis=0)
            win = slice(blk * n, blk * (n + 2))
            v1 = jnp.concatenate([jnp.where(not_sink_row, vbuf[win, :], zero_kv), ones_cols], axis=1)
            for g, kbuf in enumerate((kbuf0, kbuf1)):
                chains.append(dict(n=n, g=g, q=q_st, v1=v1,
                                   kb=jnp.where(not_sink_row, kbuf[win, :], zero_kv)))
        for ch in chains:
            bias = bias_tab[first_tab, ch['g']] if ch['n'] == 0 else bias_tab[1, ch['g']]
            ch['sc'] = _dot_nt(ch['q'], ch['kb']) + bias
        yield
        for ch in chains:
            ch['m'] = jnp.max(ch['sc'], axis=-1, keepdims=True)
        yield
        for ch in chains:
            ch['e'] = _bf(jnp.exp(ch['sc'] - ch['m']))
        yield
        for ch in chains:
            ch['o'] = _dot(ch['e'], ch['v1'])
        yield
        for n in range(n0, n0 + nstep):
            c0, c1 = [ch for ch in chains if ch['n'] == n]
            num = jnp.where(out_lo, c0['o'][:, :LANES], c1['o'][:, :LANES])
            den = jnp.where(out_lo, c0['o'][:, LANES:], c1['o'][:, LANES:])
            o = num / den
            for b in range(nq):
                o_ref[0, blk * n:blk * (n + 1), LANES * b:LANES * (b + 1)] = _bf(o[blk * b:blk * (b + 1), :])
    for buf in (kbuf0, kbuf1, vbuf):
        buf[0:blk, :] = buf[tq:tq + blk, :]


def _interleave(gens, background=()):
    while gens:
        gens = [g_ for g_ in gens if next(g_, 'done') != 'done']
        for g_ in background:
            next(g_, 'done')


N_RWKV_IN = 12
N_RWKV_SCRATCH = 11


def _mixers_kernel(x_ref, g1_ref, *refs):
    rw_in, refs = refs[:N_RWKV_IN], refs[N_RWKV_IN:]
    (watt_ref, qg_ref, kg_ref, cos_ref, sin_ref, sink_ref, wpc_ref, poolw_ref, pscale_ref,
     convw_ref, oa_ref, od_ref, ob_ref, oc_ref) = refs[:14]
    rw_scr, (kbuf0, kbuf1, vbuf, bias_tab, ubuf, cbuf) = refs[14:14 + N_RWKV_SCRATCH], refs[14 + N_RWKV_SCRATCH:]
    j = pl.program_id(1)

    @pl.when(j == 0)
    def _():
        _rwkv_init(rw_scr[0], rw_scr[-1])
        _attn_init(sink_ref, kbuf0, kbuf1, vbuf, bias_tab)
        _poolconv_init(ubuf, cbuf)

    h = _bf(_rmsnorm(x_ref[0], g1_ref[...]))
    attn = _attn_stages(j, h, watt_ref, qg_ref, kg_ref, cos_ref, sin_ref, od_ref, kbuf0, kbuf1, vbuf, bias_tab)
    poolconv = _poolconv_stages(j, h, wpc_ref, poolw_ref, pscale_ref, convw_ref, ob_ref, oc_ref, ubuf, cbuf)
    _rwkv_body(h, *rw_in, oa_ref, *rw_scr, background={1: [attn], 2: [poolconv]})


def _mixers_call(x, g1, rw_in, watt, qg, kg, cos, sin, sink_rows, wpc, poolw, pscale, convw, ts):
    b, s, _ = x.shape
    const = lambda shape: pl.BlockSpec(shape, lambda i, j: (0,) * len(shape))
    tile = lambda w: pl.BlockSpec((1, ts, w), lambda i, j: (i, j, 0))
    out = lambda w: jax.ShapeDtypeStruct((b, s, w), BF16)
    row = const((1, RW_W))
    scr = lambda: pltpu.VMEM((ts, RW_W), F32)
    assert len(rw_in) == N_RWKV_IN
    return pl.pallas_call(
        _mixers_kernel,
        out_shape=(out(RW_W), out(ATT_HQ * ATT_HD), out(POOL_W), out(CONV_W)),
        grid=(b, s // ts),
        in_specs=[tile(D_MODEL), const((1, D_MODEL)),
                  const((D_MODEL, RW_COLS)), const((1, RW_COLS)),
                  const((LANES, RW_W)), const((LANES, RW_W)), const((GATE_LORA, RW_W)),
                  row, row, row, row, row, row, row,
                  const((D_MODEL, ATT_COLS)),
                  const((1, LANES)), const((1, LANES)), tile(LANES), tile(LANES),
                  const((ATT_HKV, ATT_G * ATT_BLOCK, 1)),
                  const((D_MODEL, POOL_W + 3 * CONV_W)),
                  const((len(POOL_WINDOWS), POOL_GW, POOL_GW)), const((1, POOL_W)),
                  const((CONV_K, CONV_W))],
        out_specs=(tile(RW_W), tile(ATT_HQ * ATT_HD), tile(POOL_W), tile(CONV_W)),
        scratch_shapes=[pltpu.VMEM((ts + SUBLANES, RW_COLS), F32),
                        scr(), scr(), scr(), scr(), scr(), scr(), scr(), scr(), scr(),
                        pltpu.VMEM((RW_W // LANES, LANES, LANES), F32),
                        pltpu.VMEM((ts + ATT_BLOCK, LANES), BF16),
                        pltpu.VMEM((ts + ATT_BLOCK, LANES), BF16),
                        pltpu.VMEM((ts + ATT_BLOCK, LANES), BF16),
                        pltpu.VMEM((2, ATT_HKV, ATT_G * ATT_BLOCK, 2 * ATT_BLOCK), F32),
                        pltpu.VMEM((ts + POOL_MAXW, POOL_W), F32),
                        pltpu.VMEM((ts + SUBLANES, CONV_W), F32)],
        compiler_params=pltpu.CompilerParams(dimension_semantics=("arbitrary", "arbitrary"),
                                             vmem_limit_bytes=VMEM_LIMIT),
        name="token_mixers",
    )(x, g1, *rw_in, watt, qg, kg, cos, sin, sink_rows, wpc, poolw, pscale, convw)


def _merge_kernel(x_ref, g1_ref, wg_ref, za_ref, zb_ref, zc_ref, zd_ref,
                  wa_ref, wb_ref, wc_ref, wd_ref, wo_ref, o_ref):
    x = x_ref[...]
    h = _bf(_rmsnorm(x, g1_ref[...]))
    mixed = None
    for b, (z_ref, w_ref) in enumerate(((za_ref, wa_ref), (zb_ref, wb_ref), (zc_ref, wc_ref), (zd_ref, wd_ref))):
        gate = jax.nn.sigmoid(_dot(h, wg_ref[:, D_MODEL * b:D_MODEL * (b + 1)]))
        term = gate * _dot(z_ref[...], w_ref[...])
        mixed = term if mixed is None else mixed + term
    o_ref[...] = x + _dot(_bf(mixed), wo_ref[...])


def _merge_call(x2, g1, wg, za, zb, zc, zd, wa, wb, wc, wd, wo, tm):
    t = x2.shape[0]
    const = lambda shape: pl.BlockSpec(shape, lambda i: (0,) * len(shape))
    tile = lambda w: pl.BlockSpec((tm, w), lambda i: (i, 0))
    wout = const((RW_W, D_MODEL))
    return pl.pallas_call(
        _merge_kernel,
        out_shape=jax.ShapeDtypeStruct((t, D_MODEL), F32),
        grid=(t // tm,),
        in_specs=[tile(D_MODEL), const((1, D_MODEL)), const((D_MODEL, N_BRANCH * D_MODEL)),
                  tile(RW_W), tile(POOL_W), tile(CONV_W), tile(ATT_HQ * ATT_HD),
                  wout, wout, wout, wout, const((D_MODEL, D_MODEL))],
        out_specs=tile(D_MODEL),
        compiler_params=pltpu.CompilerParams(dimension_semantics=("arbitrary",),
                                             vmem_limit_bytes=VMEM_LIMIT),
        name="merge_mixers",
    )(x2, g1, wg, za, zb, zc, zd, wa, wb, wc, wd, wo)


def _ffn_kernel(x_ref, g2_ref, wg_ref, wu_ref, wd_ref, o_ref):
    x = x_ref[...]
    h = _bf(_rmsnorm(x, g2_ref[...]))
    acc = x
    for c in range(D_FF // FFN_CHUNK):
        cs = slice(FFN_CHUNK * c, FFN_CHUNK * (c + 1))
        gt = _dot(h, wg_ref[:, cs])
        up = _dot(h, wu_ref[:, cs])
        act = _bf(gt * jax.nn.sigmoid(gt) * up)
        acc = acc + _dot(act, wd_ref[cs, :])
    o_ref[...] = acc


def _ffn_call(x2, g2, wg, wu, wd, tm):
    t = x2.shape[0]
    const = lambda shape: pl.BlockSpec(shape, lambda i: (0,) * len(shape))
    tile = pl.BlockSpec((tm, D_MODEL), lambda i: (i, 0))
    return pl.pallas_call(
        _ffn_kernel,
        out_shape=jax.ShapeDtypeStruct((t, D_MODEL), F32),
        grid=(t // tm,),
        in_specs=[tile, const((1, D_MODEL)), const((D_MODEL, D_FF)), const((D_MODEL, D_FF)),
                  const((D_FF, D_MODEL))],
        out_specs=tile,
        compiler_params=pltpu.CompilerParams(dimension_semantics=("arbitrary",),
                                             vmem_limit_bytes=VMEM_LIMIT),
        name="ffn_swiglu",
    )(x2, g2, wg, wu, wd)


def _attn_perms():
    half = ATT_HD // 2
    q_cols = []
    for jb in range(ATT_G):
        for hf in range(2):
            for ab in range(ATT_HKV):
                head = ATT_G * ab + jb
                q_cols += [ATT_HD * head + half * hf + i for i in range(half)]
    k_cols = []
    for hf in range(2):
        for g in range(ATT_HKV):
            k_cols += [ATT_HD * g + half * hf + i for i in range(half)]
    gain_idx = [half * ((l % LANES) // (LANES // 2)) + l % half for l in range(LANES)]
    o_rows = []
    for jb in range(ATT_G):
        for ab in range(ATT_HKV):
            head = ATT_G * ab + jb
            o_rows += [ATT_HD * head + c for c in range(ATT_HD)]
    return q_cols, k_cols, gain_idx, o_rows


def kernel(x, positions, norm1_g, w_in, shift_mu, w_decay_up, w0, a_up, a0, g_up, k_k, k_a, r_k, lnx_g, lnx_b, w_rwkv_out, pool_w, pool_scale, w_pool_out, conv_w, w_conv_out, q_norm_g, k_norm_g, sinks, w_attn_out, w_o, norm2_g, w_ffn_gate, w_ffn_up, w_ffn_down):
    b, s, d = x.shape
    assert d == D_MODEL and s % ATT_BLOCK == 0
    ts = min(SEQ_TILE, s)
    tm = min(TOK_TILE, b * s)
    assert s % ts == 0 and (b * s) % tm == 0
    depth = w_in.shape[0]

    q_cols, k_cols, gain_idx, o_rows = _attn_perms()
    q_cols = jnp.asarray(q_cols, jnp.int32)
    k_cols = jnp.asarray(k_cols, jnp.int32)
    gain_idx = jnp.asarray(gain_idx, jnp.int32)
    o_rows = jnp.asarray(o_rows, jnp.int32)
    c_rw, c_pool, c_conv, c_att = RW_COLS, RW_COLS + POOL_W, RW_COLS + POOL_W + 3 * CONV_W, \
        RW_COLS + POOL_W + 3 * CONV_W + ATT_COLS

    cos, sin = _rope_tables(positions, ts)
    row = lambda v: v.reshape(1, -1).astype(F32)
    zeros_lora = jnp.zeros((DECAY_LORA, RW_W), F32)

    for i in range(depth):
        g1 = row(norm1_g[i])
        wi = w_in[i]
        wdp = _bf(jnp.concatenate([w_decay_up[i], zeros_lora], axis=0))
        wap = _bf(jnp.concatenate([zeros_lora, a_up[i]], axis=0))
        rw_in = (_bf(wi[:, :c_rw]), row(shift_mu[i]), wdp, wap, _bf(g_up[i]),
                 row(w0[i]), row(a0[i]), row(k_k[i]), row(k_a[i]), row(r_k[i]),
                 row(lnx_g[i]), row(lnx_b[i]))
        w_att = wi[:, c_conv:c_att]
        w_att = jnp.concatenate([w_att[:, q_cols], w_att[:, ATT_HQ * ATT_HD + k_cols],
                                 w_att[:, (ATT_HQ + ATT_HKV) * ATT_HD:]], axis=1)
        sink_rows = jnp.repeat(sinks[i].astype(F32).reshape(ATT_HKV, ATT_G), ATT_BLOCK, axis=1)[..., None]
        za, zd, zb, zc = _mixers_call(
            x, g1, rw_in, _bf(w_att), row(q_norm_g[i][gain_idx]), row(k_norm_g[i][gain_idx]),
            cos, sin, sink_rows, _bf(wi[:, c_rw:c_conv]), _bf(pool_w[i]),
            row(pool_scale[i]), conv_w[i].astype(F32), ts)
        flat = lambda z: z.reshape(b * s, z.shape[-1])
        x1 = _merge_call(flat(x), g1, _bf(wi[:, c_att:]), flat(za), flat(zb), flat(zc), flat(zd),
                         _bf(w_rwkv_out[i]), _bf(w_pool_out[i]), _bf(w_conv_out[i]),
                         _bf(w_attn_out[i][o_rows, :]), _bf(w_o[i]), tm)
        x2 = _ffn_call(x1, row(norm2_g[i]), _bf(w_ffn_gate[i]), _bf(w_ffn_up[i]), _bf(w_ffn_down[i]), tm)
        x = x2.reshape(b, s, d)
    return x
```

```python
import functools
import math

import jax
import jax.numpy as jnp
from jax import lax
from jax.experimental import pallas as pl
from jax.experimental.pallas import tpu as pltpu

F32 = jnp.float32
BF16 = jnp.bfloat16

D_MODEL = 1024
RW_HEADS = 8
HEAD_DIM = 64
RW_W = RW_HEADS * HEAD_DIM
DECAY_LORA = 64
ICLR_LORA = 64
GATE_LORA = 128
LNX_EPS = 64e-5
RW_COLS = 3 * RW_W + DECAY_LORA + ICLR_LORA + GATE_LORA
POOL_W = 512
POOL_GW = 128
POOL_WINDOWS = (2, 4, 8, 16)
POOL_MAXW = 16
CONV_W = 512
CONV_K = 3
ATT_HQ = 8
ATT_HKV = 2
ATT_G = ATT_HQ // ATT_HKV
ATT_HD = 64
ATT_BLOCK = 128
ATT_COLS = (ATT_HQ + 2 * ATT_HKV) * ATT_HD
ROPE_THETA = 10000.0
N_BRANCH = 4
D_FF = 2816
NORM_EPS = 1e-6

LANES = 128
SUBLANES = 8
WKV_CHUNK = 64
WKV_CHUNKS_PER_STEP = 4
ATT_BLOCKS_PER_STEP = 2
SEQ_TILE = 512
TOK_TILE = 1024
FFN_CHUNK = 256
VMEM_LIMIT = 48 * 1024 * 1024


def _bf(x):
    return x.astype(BF16)


def _dot(a, b):
    return jnp.dot(a, b, preferred_element_type=F32)


def _dot_nt(a, b):
    return lax.dot_general(a, b, (((1,), (1,)), ((), ())), preferred_element_type=F32)


def _split2(x):
    hi = _bf(x)
    lo = _bf(x - hi.astype(F32))
    return hi, lo


def _dot_x2(x, m):
    hi, lo = _split2(x)
    return _dot(hi, m) + _dot(lo, m)


def _rmsnorm(x, g):
    ms = jnp.mean(x * x, axis=-1, keepdims=True)
    return x * lax.rsqrt(ms + NORM_EPS) * g


def _iota(shape, dim):
    return lax.broadcasted_iota(jnp.int32, shape, dim)


def _rope_kernel(pos_ref, inv_ref, cos_ref, sin_ref):
    ang = pos_ref[0].astype(F32) * inv_ref[...]
    lane = _iota(ang.shape, 1)
    cos_ref[0] = jnp.cos(ang)
    s = jnp.sin(ang)
    sin_ref[0] = jnp.where(lane < LANES // 2, -s, s)


def _rope_tables(positions, ts):
    b, s = positions.shape
    half = ATT_HD // 2
    inv = ROPE_THETA ** (-jnp.arange(half, dtype=F32) * 2.0 / ATT_HD)
    inv = jnp.tile(inv, LANES // half)[None, :]
    pos3 = positions.reshape(b, s, 1)
    out = jax.ShapeDtypeStruct((b, s, LANES), F32)
    return pl.pallas_call(
        _rope_kernel,
        out_shape=(out, out),
        grid=(b, s // ts),
        in_specs=[pl.BlockSpec((1, ts, 1), lambda i, j: (i, j, 0)),
                  pl.BlockSpec((1, LANES), lambda i, j: (0, 0))],
        out_specs=(pl.BlockSpec((1, ts, LANES), lambda i, j: (i, j, 0)),
                   pl.BlockSpec((1, ts, LANES), lambda i, j: (i, j, 0))),
        compiler_params=pltpu.CompilerParams(dimension_semantics=("arbitrary", "arbitrary")),
        name="rope_tables",
    )(pos3, inv)


def _rwkv_init(pbuf, st_s):
    st_s[...] = jnp.zeros(st_s.shape, F32)
    pbuf[0:SUBLANES, :] = jnp.zeros((SUBLANES, RW_COLS), F32)


def _rwkv_body(h, wrw_ref, mu_ref, wdp_ref, wap_ref, gup_ref, w0_ref, a0_ref,
               kk_ref, ka_ref, rk_ref, lng_ref, lnb_ref, o_ref,
               pbuf, r_s, k_s, v_s, kap_s, beta_s, lw_s, bonus_s, g_s, y_s, st_s, background):
    tb = h.shape[0]
    L = WKV_CHUNK

    gi = _iota((LANES, LANES), 0) // HEAD_DIM
    gj = _iota((LANES, LANES), 1) // HEAD_DIM
    seg_ones = jnp.where(gi == gj, 1.0, 0.0).astype(BF16)

    def segsum(z):
        return jnp.concatenate(
            [_dot(_bf(z[:, LANES * b:LANES * (b + 1)]), seg_ones) for b in range(RW_W // LANES)], axis=1)

    p = _dot(h, wrw_ref[...])
    pbuf[SUBLANES:SUBLANES + tb, :] = p
    p_prev = pbuf[SUBLANES - 1:SUBLANES - 1 + tb, :]
    pbuf[SUBLANES - 1:SUBLANES, :] = p[tb - 1:tb, :]
    pm = p + (p_prev - p) * mu_ref[...]
    r = pm[:, 0:RW_W]
    k = pm[:, RW_W:2 * RW_W]
    v = pm[:, 2 * RW_W:3 * RW_W]
    lora_in = pm[:, 3 * RW_W:3 * RW_W + LANES]
    gd = pm[:, 3 * RW_W + LANES:RW_COLS]
    z = w0_ref[...] + _dot(_bf(jnp.tanh(lora_in)), wdp_ref[...])
    lw_s[...] = (-math.exp(-0.5)) * jax.nn.sigmoid(z)
    a = jax.nn.sigmoid(a0_ref[...] + _dot(_bf(lora_in), wap_ref[...]))
    g_s[...] = _dot(_bf(jax.nn.sigmoid(gd)), gup_ref[...])
    kk = k * kk_ref[...]
    kap = kk * lax.rsqrt(jnp.maximum(segsum(kk * kk), 1e-24))
    k2 = k * (1.0 + (a - 1.0) * ka_ref[...])
    r_s[...] = r
    k_s[...] = k2
    v_s[...] = v
    kap_s[...] = kap
    beta_s[...] = kap * a
    bonus_s[...] = segsum(r * k2 * rk_ref[...]) * v

    assert L == HEAD_DIM
    lane_lo = _iota((L, LANES), 1) < HEAD_DIM
    tok = _iota((L, LANES), 0)
    col = _iota((L, LANES), 1) % L
    strict = col < tok
    incl = col <= tok
    eye_sbs = jnp.where(col == tok, 1.0, 0.0)
    same_head = (_iota((LANES, LANES), 0) // HEAD_DIM) == (_iota((LANES, LANES), 1) // HEAD_DIM)
    ltri = jnp.where(_iota((L, L), 1) <= _iota((L, L), 0), 1.0, 0.0).astype(BF16)

    def stack(zz):
        zero = jnp.zeros_like(zz)
        return jnp.concatenate([jnp.where(lane_lo, zz, zero), jnp.where(lane_lo, zero, zz)], axis=0)

    n_pairs = RW_W // LANES
    n_ch = min(WKV_CHUNKS_PER_STEP, tb // L)

    def phase1(gidx, chains):
        for q in range(n_ch):
            t0 = (gidx * n_ch + q) * L
            rows = slice(t0, t0 + L)
            lwc = lw_s[rows, :]
            l1, l2 = _split2(lwc)
            c = _dot(ltri, l1) + _dot(ltri, l2)
            c_last = c[L - 1:L, :]
            e_in = jnp.exp(c)
            e_prev = jnp.exp(c - lwc)
            e_out = jnp.exp(-c)
            g_end = jnp.exp(c_last)
            e_end = g_end * e_out
            kc = k_s[rows, :]
            vc = _bf(v_s[rows, :])
            betac = beta_s[rows, :]
            rt = _bf(r_s[rows, :] * e_in)
            kt = _bf(kc * e_out)
            bt = _bf(betac * e_out)
            kapt = _bf(kap_s[rows, :] * e_prev)
            khat = kc * e_end
            bhat = betac * e_end
            for pr in range(n_pairs):
                sl = slice(LANES * pr, LANES * (pr + 1))
                kts = stack(kt[:, sl])
                bts = stack(bt[:, sl])
                chains.append(dict(
                    q=q, pr=pr, rows=rows, sl=sl, rt=rt[:, sl], kapt=kapt[:, sl], v=vc[:, sl],
                    kb=jnp.concatenate([bts, kts], axis=0), kapts=stack(kapt[:, sl]),
                    vs=stack(vc[:, sl]), khat=khat[:, sl], bhat=bhat[:, sl], g_end=g_end[:, sl]))
            yield
        for ch in chains:
            ma = _dot_nt(jnp.concatenate([ch['kapt'], ch['rt']], axis=0), ch['kb'])
            ch['m_ab'] = jnp.where(strict, ma[:L, :LANES], 0.0)
            ch['m_ak'] = _bf(jnp.where(strict, ma[:L, LANES:], 0.0))
            ch['a_qb'] = _bf(jnp.where(incl, ma[L:, :LANES], 0.0))
            ch['a_qk'] = _bf(jnp.where(incl, ma[L:, LANES:], 0.0))
        yield
        for ch in chains:
            ch['t'] = eye_sbs - ch['m_ab']
            ch['pw'] = _bf(ch['m_ab'])
        for ch in chains:
            ch['pw'] = _bf(_dot(ch['pw'], stack(ch['pw'])))
        yield
        for it in range(5):
            for ch in chains:
                ch['pws'] = stack(ch['pw'])
            if it < 4:
                for ch in chains:
                    both = _dot(jnp.concatenate([ch['pw'], _bf(ch['t'])], axis=0), ch['pws'])
                    ch['pw_next'] = both[:L]
                    ch['t'] = ch['t'] + both[L:]
            else:
                for ch in chains:
                    ch['t'] = ch['t'] + _dot(_bf(ch['t']), ch['pws'])
            if it == 0:
                for ch in chains:
                    mvy = _dot(jnp.concatenate([ch['m_ak'], ch['a_qk']], axis=0), ch['vs'])
                    ch['mvs'] = stack(_bf(mvy[:L]))
                    ch['y0'] = mvy[L:]
            if it == 1:
                for ch in chains:
                    ch['khat_t'] = _bf(ch['khat'].T)
                    ch['bhat_t'] = _bf(ch['bhat'].T)
                    ch['g_rows'] = jnp.broadcast_to(ch['g_end'], (LANES, LANES)).T
            if it == 2:
                for ch in chains:
                    ch['kv0'] = _dot(ch['khat_t'], ch['v'])
            if it < 4:
                for ch in chains:
                    ch['pw'] = _bf(ch['pw_next'])
            yield
        for ch in chains:
            wu = _dot(_bf(ch['t']), jnp.concatenate([ch['kapts'], ch['mvs']], axis=1))
            ch['u0'] = wu[:, LANES:]
            ch['wr'] = jnp.concatenate([_bf(wu[:, :LANES]), ch['rt']], axis=0)
        yield

    def phase2(chains, st):
        for q in range(n_ch):
            cq = [ch for ch in chains if ch['q'] == q]
            st_b = [_bf(s_) for s_ in st]
            ws = [_dot(ch['wr'], st_b[ch['pr']]) for ch in cq]
            u_b = [_bf(ws[ch['pr']][:L] + ch['u0']) for ch in cq]
            yield
            st[:] = [ch['g_rows'] * st[ch['pr']]
                     + jnp.where(same_head, ch['kv0'] - _dot(ch['bhat_t'], u_b[ch['pr']]), 0.0) for ch in cq]
            for ch in cq:
                y_s[ch['rows'], ch['sl']] = (ws[ch['pr']][L:] + ch['y0']
                                             - _dot(ch['a_qb'], stack(u_b[ch['pr']])))
            yield

    def stage_c(gidx):
        rows = slice(gidx * n_ch * L, (gidx + 1) * n_ch * L)
        y = y_s[rows, :]
        mean = segsum(y) * (1.0 / HEAD_DIM)
        yield
        d = y - mean
        var = segsum(d * d) * (1.0 / HEAD_DIM)
        yield
        yn = d * lax.rsqrt(var + LNX_EPS) * lng_ref[...] + lnb_ref[...]
        o_ref[0, rows, :] = _bf((yn + bonus_s[rows, :]) * g_s[rows, :])
        yield

    st = [st_s[pr] for pr in range(n_pairs)]
    chains_of = {}
    n_groups = tb // (L * n_ch)
    for step in range(n_groups + 2):
        gens = []
        if step < n_groups:
            chains_of[step] = []
            gens.append(phase1(step, chains_of[step]))
        if 0 <= step - 1 < n_groups:
            gens.append(phase2(chains_of.pop(step - 1), st))
        if 0 <= step - 2 < n_groups:
            gens.append(stage_c(step - 2))
        _interleave(gens, background.get(step, ()))
    for pr in range(n_pairs):
        st_s[pr] = st[pr]
    _interleave([g_ for gs_ in background.values() for g_ in gs_])


def _poolconv_init(ubuf, sbuf_a, sbuf_b, cbuf):
    for buf in (ubuf, sbuf_a, sbuf_b):
        buf[0:2 * POOL_MAXW, :] = jnp.zeros((2 * POOL_MAXW, POOL_W), F32)
    cbuf[0:SUBLANES, :] = jnp.zeros((SUBLANES, CONV_W), F32)


def _poolconv_stages(j, h, wpc_ref, poolw_ref, pscale_ref, convw_ref, ob_ref, oc_ref,
                     ubuf, sbuf_a, sbuf_b, cbuf):
    tb = h.shape[0]
    pad = POOL_MAXW
    lo = 2 * POOL_MAXW
    u = _dot(h, wpc_ref[:, 0:POOL_W])
    ubuf[lo:lo + tb, :] = u
    yield
    assert POOL_WINDOWS == tuple(2 ** (gi + 1) for gi in range(len(POOL_WINDOWS)))
    src = ubuf
    sums = []
    for gi, win in enumerate(POOL_WINDOWS):
        dst = sbuf_a if gi % 2 == 0 else sbuf_b
        cols = slice(POOL_GW * gi, POOL_W)
        shift = win // 2
        dst[pad:lo + tb, cols] = src[pad:lo + tb, cols] + src[pad - shift:lo + tb - shift, cols]
        sums.append(dst)
        src = dst
        yield
    t_glob = j * tb + _iota((tb, 1), 0)
    for gi, win in enumerate(POOL_WINDOWS):
        cs = slice(POOL_GW * gi, POOL_GW * (gi + 1))
        cnt = jnp.minimum(t_glob + 1, win).astype(F32)
        zc = sums[gi][lo:lo + tb, cs] / cnt - u[:, cs]
        zz = _dot(_bf(zc), poolw_ref[gi])
        ob_ref[0, :, cs] = _bf(zz * pscale_ref[:, cs])
        yield
    ubuf[pad:lo, :] = ubuf[tb + pad:tb + lo, :]

    cg = _dot(h, wpc_ref[:, POOL_W + CONV_W:POOL_W + 2 * CONV_W])
    yield
    cu = _dot(h, wpc_ref[:, POOL_W + 2 * CONV_W:POOL_W + 3 * CONV_W])
    vv = cg * cu
    cbuf[SUBLANES:SUBLANES + tb, :] = vv
    yield
    bg = _dot(h, wpc_ref[:, POOL_W:POOL_W + CONV_W])
    conv = (convw_ref[0:1, :] * cbuf[SUBLANES - 2:SUBLANES - 2 + tb, :]
            + convw_ref[1:2, :] * cbuf[SUBLANES - 1:SUBLANES - 1 + tb, :]
            + convw_ref[2:3, :] * vv)
    cbuf[0:SUBLANES, :] = cbuf[tb:tb + SUBLANES, :]
    oc_ref[0] = _bf(bg * conv)
    yield


def _attn_init(sink_ref, kbuf0, kbuf1, vbuf, bias_tab):
    blk = ATT_BLOCK
    rows = ATT_G * blk
    kbuf0[0:blk, :] = jnp.zeros((blk, LANES), BF16)
    kbuf1[0:blk, :] = jnp.zeros((blk, LANES), BF16)
    vbuf[0:blk, :] = jnp.zeros((blk, LANES), BF16)
    qi = _iota((rows, 2 * blk), 0) % blk + blk
    kj = _iota((rows, 2 * blk), 1)
    dist = qi - kj
    band = (dist >= 0) & (dist < ATT_BLOCK)
    for g in range(ATT_HKV):
        sink = sink_ref[g]
        bias_tab[0, g] = jnp.where(kj == 0, sink, jnp.where(band & (kj >= blk), 0.0, -jnp.inf))
        bias_tab[1, g] = jnp.where(kj == 0, sink, jnp.where(band, 0.0, -jnp.inf))


def _attn_stages(j, h, watt_ref, qg_ref, kg_ref, cos_ref, sin_ref, o_ref, kbuf0, kbuf1, vbuf, bias_tab):
    tq = h.shape[0]
    blk = ATT_BLOCK
    nq = ATT_HQ * ATT_HD // LANES
    rows = nq * blk

    hi_ = (_iota((LANES, LANES), 0) % ATT_HD) // (ATT_HD // 2)
    hj_ = (_iota((LANES, LANES), 1) % ATT_HD) // (ATT_HD // 2)
    seg_mean = jnp.where(hi_ == hj_, 1.0 / ATT_HD, 0.0).astype(BF16)

    p = _dot(h, watt_ref[...])
    yield
    cos = cos_ref[0]
    sin = sin_ref[0]

    def norm_rope(xb, gain):
        ms = _dot_x2(xb * xb, seg_mean)
        yb = xb * lax.rsqrt(ms + NORM_EPS) * gain
        return yb * cos + pltpu.roll(yb, LANES // 2, axis=1) * sin

    kn = norm_rope(p[:, nq * LANES:(nq + 1) * LANES], kg_ref[...])
    kv_lane = (_iota((tq, LANES), 1) % ATT_HD) // (ATT_HD // 2)
    kbuf0[blk:blk + tq, :] = _bf(jnp.where(kv_lane == 0, kn, 0.0))
    kbuf1[blk:blk + tq, :] = _bf(jnp.where(kv_lane == 1, kn, 0.0))
    vbuf[blk:blk + tq, :] = _bf(p[:, (nq + 1) * LANES:(nq + 2) * LANES])
    yield
    qs = [_bf(norm_rope(p[:, LANES * b:LANES * (b + 1)], qg_ref[...]) * (ATT_HD ** -0.5))
          for b in range(nq)]
    yield

    out_lo = _iota((rows, LANES), 1) < ATT_HD
    not_sink_row = _iota((2 * blk, LANES), 0) > 0
    ones_cols = jnp.ones((2 * blk, LANES), BF16)
    zero_kv = jnp.zeros((2 * blk, LANES), BF16)
    first_tab = jnp.where(j == 0, 0, 1)
    nblk = tq // blk
    nstep = min(ATT_BLOCKS_PER_STEP, nblk)
    for n0 in range(0, nblk, nstep):
        chains = []
        for n in range(n0, n0 + nstep):
            q_st = jnp.concatenate([q[blk * n:blk * (n + 1), :] for q in qs], axis=0)
            win = slice(blk * n, blk * (n + 2))
            v1 = jnp.concatenate([jnp.where(not_sink_row, vbuf[win, :], zero_kv), ones_cols], axis=1)
            for g, kbuf in enumerate((kbuf0, kbuf1)):
                chains.append(dict(n=n, g=g, q=q_st, v1=v1,
                                   kb=jnp.where(not_sink_row, kbuf[win, :], zero_kv)))
        for ch in chains:
            bias = bias_tab[first_tab, ch['g']] if ch['n'] == 0 else bias_tab[1, ch['g']]
            ch['sc'] = _dot_nt(ch['q'], ch['kb']) + bias
        yield
        for ch in chains:
            ch['m'] = jnp.max(ch['sc'], axis=-1, keepdims=True)
        yield
        for ch in chains:
            ch['e'] = _bf(jnp.exp(ch['sc'] - ch['m']))
        yield
        for ch in chains:
            ch['o'] = _dot(ch['e'], ch['v1'])
        yield
        for n in range(n0, n0 + nstep):
            c0, c1 = [ch for ch in chains if ch['n'] == n]
            num = jnp.where(out_lo, c0['o'][:, :LANES], c1['o'][:, :LANES])
            den = jnp.where(out_lo, c0['o'][:, LANES:], c1['o'][:, LANES:])
            o = num / den
            for b in range(nq):
                o_ref[0, blk * n:blk * (n + 1), LANES * b:LANES * (b + 1)] = _bf(o[blk * b:blk * (b + 1), :])
    for buf in (kbuf0, kbuf1, vbuf):
        buf[0:blk, :] = buf[tq:tq + blk, :]


def _interleave(gens, background=()):
    while gens:
        gens = [g_ for g_ in gens if next(g_, 'done') != 'done']
        for g_ in background:
            next(g_, 'done')


N_RWKV_IN = 12
N_RWKV_SCRATCH = 11


def _mixers_kernel(x_ref, g1_ref, *refs):
    rw_in, refs = refs[:N_RWKV_IN], refs[N_RWKV_IN:]
    (watt_ref, qg_ref, kg_ref, cos_ref, sin_ref, sink_ref, wpc_ref, poolw_ref, pscale_ref,
     convw_ref, oa_ref, od_ref, ob_ref, oc_ref) = refs[:14]
    rw_scr = refs[14:14 + N_RWKV_SCRATCH]
    kbuf0, kbuf1, vbuf, bias_tab, ubuf, sbuf_a, sbuf_b, cbuf = refs[14 + N_RWKV_SCRATCH:]
    j = pl.program_id(1)

    @pl.when(j == 0)
    def _():
        _rwkv_init(rw_scr[0], rw_scr[-1])
        _attn_init(sink_ref, kbuf0, kbuf1, vbuf, bias_tab)
        _poolconv_init(ubuf, sbuf_a, sbuf_b, cbuf)

    h = _bf(_rmsnorm(x_ref[0], g1_ref[...]))
    attn = _attn_stages(j, h, watt_ref, qg_ref, kg_ref, cos_ref, sin_ref, od_ref, kbuf0, kbuf1, vbuf, bias_tab)
    poolconv = _poolconv_stages(j, h, wpc_ref, poolw_ref, pscale_ref, convw_ref, ob_ref, oc_ref,
                                ubuf, sbuf_a, sbuf_b, cbuf)
    _rwkv_body(h, *rw_in, oa_ref, *rw_scr, background={1: [attn], 2: [poolconv]})


def _mixers_call(x, g1, rw_in, watt, qg, kg, cos, sin, sink_rows, wpc, poolw, pscale, convw, ts):
    b, s, _ = x.shape
    const = lambda shape: pl.BlockSpec(shape, lambda i, j: (0,) * len(shape))
    tile = lambda w: pl.BlockSpec((1, ts, w), lambda i, j: (i, j, 0))
    out = lambda w: jax.ShapeDtypeStruct((b, s, w), BF16)
    row = const((1, RW_W))
    scr = lambda: pltpu.VMEM((ts, RW_W), F32)
    assert len(rw_in) == N_RWKV_IN
    return pl.pallas_call(
        _mixers_kernel,
        out_shape=(out(RW_W), out(ATT_HQ * ATT_HD), out(POOL_W), out(CONV_W)),
        grid=(b, s // ts),
        in_specs=[tile(D_MODEL), const((1, D_MODEL)),
                  const((D_MODEL, RW_COLS)), const((1, RW_COLS)),
                  const((LANES, RW_W)), const((LANES, RW_W)), const((GATE_LORA, RW_W)),
                  row, row, row, row, row, row, row,
                  const((D_MODEL, ATT_COLS)),
                  const((1, LANES)), const((1, LANES)), tile(LANES), tile(LANES),
                  const((ATT_HKV, ATT_G * ATT_BLOCK, 1)),
                  const((D_MODEL, POOL_W + 3 * CONV_W)),
                  const((len(POOL_WINDOWS), POOL_GW, POOL_GW)), const((1, POOL_W)),
                  const((CONV_K, CONV_W))],
        out_specs=(tile(RW_W), tile(ATT_HQ * ATT_HD), tile(POOL_W), tile(CONV_W)),
        scratch_shapes=[pltpu.VMEM((ts + SUBLANES, RW_COLS), F32),
                        scr(), scr(), scr(), scr(), scr(), scr(), scr(), scr(), scr(),
                        pltpu.VMEM((RW_W // LANES, LANES, LANES), F32),
                        pltpu.VMEM((ts + ATT_BLOCK, LANES), BF16),
                        pltpu.VMEM((ts + ATT_BLOCK, LANES), BF16),
                        pltpu.VMEM((ts + ATT_BLOCK, LANES), BF16),
                        pltpu.VMEM((2, ATT_HKV, ATT_G * ATT_BLOCK, 2 * ATT_BLOCK), F32),
                        pltpu.VMEM((ts + 2 * POOL_MAXW, POOL_W), F32),
                        pltpu.VMEM((ts + 2 * POOL_MAXW, POOL_W), F32),
                        pltpu.VMEM((ts + 2 * POOL_MAXW, POOL_W), F32),
                        pltpu.VMEM((ts + SUBLANES, CONV_W), F32)],
        compiler_params=pltpu.CompilerParams(dimension_semantics=("arbitrary", "arbitrary"),
                                             vmem_limit_bytes=VMEM_LIMIT),
        name="token_mixers",
    )(x, g1, *rw_in, watt, qg, kg, cos, sin, sink_rows, wpc, poolw, pscale, convw)


def _merge_kernel(x_ref, g1_ref, wg_ref, za_ref, zb_ref, zc_ref, zd_ref,
                  wa_ref, wb_ref, wc_ref, wd_ref, wo_ref, o_ref):
    x = x_ref[...]
    h = _bf(_rmsnorm(x, g1_ref[...]))
    mixed = None
    for b, (z_ref, w_ref) in enumerate(((za_ref, wa_ref), (zb_ref, wb_ref), (zc_ref, wc_ref), (zd_ref, wd_ref))):
        gate = jax.nn.sigmoid(_dot(h, wg_ref[:, D_MODEL * b:D_MODEL * (b + 1)]))
        term = gate * _dot(z_ref[...], w_ref[...])
        mixed = term if mixed is None else mixed + term
    o_ref[...] = x + _dot(_bf(mixed), wo_ref[...])


def _merge_call(x2, g1, wg, za, zb, zc, zd, wa, wb, wc, wd, wo, tm):
    t = x2.shape[0]
    const = lambda shape: pl.BlockSpec(shape, lambda i: (0,) * len(shape))
    tile = lambda w: pl.BlockSpec((tm, w), lambda i: (i, 0))
    wout = const((RW_W, D_MODEL))
    return pl.pallas_call(
        _merge_kernel,
        out_shape=jax.ShapeDtypeStruct((t, D_MODEL), F32),
        grid=(t // tm,),
        in_specs=[tile(D_MODEL), const((1, D_MODEL)), const((D_MODEL, N_BRANCH * D_MODEL)),
                  tile(RW_W), tile(POOL_W), tile(CONV_W), tile(ATT_HQ * ATT_HD),
                  wout, wout, wout, wout, const((D_MODEL, D_MODEL))],
        out_specs=tile(D_MODEL),
        compiler_params=pltpu.CompilerParams(dimension_semantics=("arbitrary",),
                                             vmem_limit_bytes=VMEM_LIMIT),
        name="merge_mixers",
    )(x2, g1, wg, za, zb, zc, zd, wa, wb, wc, wd, wo)


def _ffn_kernel(x_ref, g2_ref, wg_ref, wu_ref, wd_ref, o_ref):
    x = x_ref[...]
    h = _bf(_rmsnorm(x, g2_ref[...]))
    acc = x
    for c in range(D_FF // FFN_CHUNK):
        cs = slice(FFN_CHUNK * c, FFN_CHUNK * (c + 1))
        gt = _dot(h, wg_ref[:, cs])
        up = _dot(h, wu_ref[:, cs])
        act = _bf(gt * jax.nn.sigmoid(gt) * up)
        acc = acc + _dot(act, wd_ref[cs, :])
    o_ref[...] = acc


def _ffn_call(x2, g2, wg, wu, wd, tm):
    t = x2.shape[0]
    const = lambda shape: pl.BlockSpec(shape, lambda i: (0,) * len(shape))
    tile = pl.BlockSpec((tm, D_MODEL), lambda i: (i, 0))
    return pl.pallas_call(
        _ffn_kernel,
        out_shape=jax.ShapeDtypeStruct((t, D_MODEL), F32),
        grid=(t // tm,),
        in_specs=[tile, const((1, D_MODEL)), const((D_MODEL, D_FF)), const((D_MODEL, D_FF)),
                  const((D_FF, D_MODEL))],
        out_specs=tile,
        compiler_params=pltpu.CompilerParams(dimension_semantics=("arbitrary",),
                                             vmem_limit_bytes=VMEM_LIMIT),
        name="ffn_swiglu",
    )(x2, g2, wg, wu, wd)


def _attn_perms():
    half = ATT_HD // 2
    q_cols = []
    for jb in range(ATT_G):
        for hf in range(2):
            for ab in range(ATT_HKV):
                head = ATT_G * ab + jb
                q_cols += [ATT_HD * head + half * hf + i for i in range(half)]
    k_cols = []
    for hf in range(2):
        for g in range(ATT_HKV):
            k_cols += [ATT_HD * g + half * hf + i for i in range(half)]
    gain_idx = [half * ((l % LANES) // (LANES // 2)) + l % half for l in range(LANES)]
    o_rows = []
    for jb in range(ATT_G):
        for ab in range(ATT_HKV):
            head = ATT_G * ab + jb
            o_rows += [ATT_HD * head + c for c in range(ATT_HD)]
    return q_cols, k_cols, gain_idx, o_rows


def kernel(x, positions, norm1_g, w_in, shift_mu, w_decay_up, w0, a_up, a0, g_up, k_k, k_a, r_k, lnx_g, lnx_b, w_rwkv_out, pool_w, pool_scale, w_pool_out, conv_w, w_conv_out, q_norm_g, k_norm_g, sinks, w_attn_out, w_o, norm2_g, w_ffn_gate, w_ffn_up, w_ffn_down):
    b, s, d = x.shape
    assert d == D_MODEL and s % ATT_BLOCK == 0
    ts = min(SEQ_TILE, s)
    tm = min(TOK_TILE, b * s)
    assert s % ts == 0 and (b * s) % tm == 0
    depth = w_in.shape[0]

    q_cols, k_cols, gain_idx, o_rows = _attn_perms()
    q_cols = jnp.asarray(q_cols, jnp.int32)
    k_cols = jnp.asarray(k_cols, jnp.int32)
    gain_idx = jnp.asarray(gain_idx, jnp.int32)
    o_rows = jnp.asarray(o_rows, jnp.int32)
    c_rw, c_pool, c_conv, c_att = RW_COLS, RW_COLS + POOL_W, RW_COLS + POOL_W + 3 * CONV_W, \
        RW_COLS + POOL_W + 3 * CONV_W + ATT_COLS

    cos, sin = _rope_tables(positions, ts)
    row = lambda v: v.reshape(1, -1).astype(F32)
    zeros_lora = jnp.zeros((DECAY_LORA, RW_W), F32)

    for i in range(depth):
        g1 = row(norm1_g[i])
        wi = w_in[i]
        wdp = _bf(jnp.concatenate([w_decay_up[i], zeros_lora], axis=0))
        wap = _bf(jnp.concatenate([zeros_lora, a_up[i]], axis=0))
        rw_in = (_bf(wi[:, :c_rw]), row(shift_mu[i]), wdp, wap, _bf(g_up[i]),
                 row(w0[i]), row(a0[i]), row(k_k[i]), row(k_a[i]), row(r_k[i]),
                 row(lnx_g[i]), row(lnx_b[i]))
        w_att = wi[:, c_conv:c_att]
        w_att = jnp.concatenate([w_att[:, q_cols], w_att[:, ATT_HQ * ATT_HD + k_cols],
                                 w_att[:, (ATT_HQ + ATT_HKV) * ATT_HD:]], axis=1)
        sink_rows = jnp.repeat(sinks[i].astype(F32).reshape(ATT_HKV, ATT_G), ATT_BLOCK, axis=1)[..., None]
        za, zd, zb, zc = _mixers_call(
            x, g1, rw_in, _bf(w_att), row(q_norm_g[i][gain_idx]), row(k_norm_g[i][gain_idx]),
            cos, sin, sink_rows, _bf(wi[:, c_rw:c_conv]), _bf(pool_w[i]),
            row(pool_scale[i]), conv_w[i].astype(F32), ts)
        flat = lambda z: z.reshape(b * s, z.shape[-1])
        x1 = _merge_call(flat(x), g1, _bf(wi[:, c_att:]), flat(za), flat(zb), flat(zc), flat(zd),
                         _bf(w_rwkv_out[i]), _bf(w_pool_out[i]), _bf(w_conv_out[i]),
                         _bf(w_attn_out[i][o_rows, :]), _bf(w_o[i]), tm)
        x2 = _ffn_call(x1, row(norm2_g[i]), _bf(w_ffn_gate[i]), _bf(w_ffn_up[i]), _bf(w_ffn_down[i]), tm)
        x = x2.reshape(b, s, d)
    return x
```

```python
import math

import jax
import jax.numpy as jnp
from jax import lax
from jax.experimental import pallas as pl
from jax.experimental.pallas import tpu as pltpu

F32 = jnp.float32
BF16 = jnp.bfloat16

D_MODEL = 1024
RW_HEADS = 8
HEAD_DIM = 64
RW_W = RW_HEADS * HEAD_DIM
DECAY_LORA = 64
ICLR_LORA = 64
GATE_LORA = 128
LNX_EPS = 64e-5
RW_COLS = 3 * RW_W + DECAY_LORA + ICLR_LORA + GATE_LORA
POOL_W = 512
POOL_GW = 128
POOL_WINDOWS = (2, 4, 8, 16)
POOL_MAXW = 16
CONV_W = 512
CONV_K = 3
ATT_HQ = 8
ATT_HKV = 2
ATT_G = ATT_HQ // ATT_HKV
ATT_HD = 64
ATT_BLOCK = 128
ATT_COLS = (ATT_HQ + 2 * ATT_HKV) * ATT_HD
ROPE_THETA = 10000.0
N_BRANCH = 4
D_FF = 2816
NORM_EPS = 1e-6

LANES = 128
SUBLANES = 8
WKV_CHUNK = 64
WKV_CHUNKS_PER_STEP = 4
ATT_BLOCKS_PER_STEP = 2
SEQ_TILE = 512
TOK_TILE = 1024
FFN_CHUNK = 256
VMEM_LIMIT = 48 * 1024 * 1024


def _bf(x):
    return x.astype(BF16)


def _dot(a, b):
    return jnp.dot(a, b, preferred_element_type=F32)


def _dot_nt(a, b):
    return lax.dot_general(a, b, (((1,), (1,)), ((), ())), preferred_element_type=F32)


def _split2(x):
    hi = _bf(x)
    lo = _bf(x - hi.astype(F32))
    return hi, lo


def _rmsnorm(x, g):
    ms = jnp.mean(x * x, axis=-1, keepdims=True)
    return x * lax.rsqrt(ms + NORM_EPS) * g


def _iota(shape, dim):
    return lax.broadcasted_iota(jnp.int32, shape, dim)


def _rope_kernel(pos_ref, inv_ref, cos_ref, sin_ref):
    ang = pos_ref[0].astype(F32) * inv_ref[...]
    lane = _iota(ang.shape, 1)
    cos_ref[0] = jnp.cos(ang)
    s = jnp.sin(ang)
    sin_ref[0] = jnp.where(lane < LANES // 2, -s, s)


def _rope_tables(positions, ts):
    b, s = positions.shape
    half = ATT_HD // 2
    inv = ROPE_THETA ** (-jnp.arange(half, dtype=F32) * 2.0 / ATT_HD)
    inv = jnp.tile(inv, LANES // half)[None, :]
    pos3 = positions.reshape(b, s, 1)
    out = jax.ShapeDtypeStruct((b, s, LANES), F32)
    return pl.pallas_call(
        _rope_kernel,
        out_shape=(out, out),
        grid=(b, s // ts),
        in_specs=[pl.BlockSpec((1, ts, 1), lambda i, j: (i, j, 0)),
                  pl.BlockSpec((1, LANES), lambda i, j: (0, 0))],
        out_specs=(pl.BlockSpec((1, ts, LANES), lambda i, j: (i, j, 0)),
                   pl.BlockSpec((1, ts, LANES), lambda i, j: (i, j, 0))),
        compiler_params=pltpu.CompilerParams(dimension_semantics=("arbitrary", "arbitrary")),
        name="rope_tables",
    )(pos3, inv)


def _rwkv_init(pbuf, st_s):
    st_s[...] = jnp.zeros(st_s.shape, F32)
    pbuf[0:SUBLANES, :] = jnp.zeros((SUBLANES, RW_COLS), F32)


def _rwkv_body(h, wrw_ref, mu_ref, wdp_ref, wap_ref, gup_ref, w0_ref, a0_ref,
               kk_ref, ka_ref, rk_ref, lng_ref, lnb_ref, o_ref,
               pbuf, r_s, k_s, v_s, kap_s, beta_s, lw_s, bonus_s, g_s, y_s, st_s, background):
    tb = h.shape[0]
    L = WKV_CHUNK

    gi = _iota((LANES, LANES), 0) // HEAD_DIM
    gj = _iota((LANES, LANES), 1) // HEAD_DIM
    seg_ones = jnp.where(gi == gj, 1.0, 0.0).astype(BF16)

    def segsum(z):
        return jnp.concatenate(
            [_dot(_bf(z[:, LANES * b:LANES * (b + 1)]), seg_ones) for b in range(RW_W // LANES)], axis=1)

    p = _dot(h, wrw_ref[...])
    pbuf[SUBLANES:SUBLANES + tb, :] = p
    p_prev = pbuf[SUBLANES - 1:SUBLANES - 1 + tb, :]
    pbuf[SUBLANES - 1:SUBLANES, :] = p[tb - 1:tb, :]
    pm = p + (p_prev - p) * mu_ref[...]
    r = pm[:, 0:RW_W]
    k = pm[:, RW_W:2 * RW_W]
    v = pm[:, 2 * RW_W:3 * RW_W]
    lora_in = pm[:, 3 * RW_W:3 * RW_W + LANES]
    gd = pm[:, 3 * RW_W + LANES:RW_COLS]
    z = w0_ref[...] + _dot(_bf(jnp.tanh(lora_in)), wdp_ref[...])
    lw_s[...] = (-math.exp(-0.5)) * jax.nn.sigmoid(z)
    a = jax.nn.sigmoid(a0_ref[...] + _dot(_bf(lora_in), wap_ref[...]))
    g_s[...] = _dot(_bf(jax.nn.sigmoid(gd)), gup_ref[...])
    kk = k * kk_ref[...]
    kap = kk * lax.rsqrt(jnp.maximum(segsum(kk * kk), 1e-24))
    k2 = k * (1.0 + (a - 1.0) * ka_ref[...])
    r_s[...] = r
    k_s[...] = k2
    v_s[...] = v
    kap_s[...] = kap
    beta_s[...] = kap * a
    bonus_s[...] = segsum(r * k2 * rk_ref[...]) * v

    assert L == HEAD_DIM
    lane_lo = _iota((L, LANES), 1) < HEAD_DIM
    tok = _iota((L, LANES), 0)
    col = _iota((L, LANES), 1) % L
    strict = col < tok
    incl = col <= tok
    eye_sbs = jnp.where(col == tok, 1.0, 0.0)
    same_head = (_iota((LANES, LANES), 0) // HEAD_DIM) == (_iota((LANES, LANES), 1) // HEAD_DIM)
    ltri = jnp.where(_iota((L, L), 1) <= _iota((L, L), 0), 1.0, 0.0).astype(BF16)

    def stack(zz):
        zero = jnp.zeros_like(zz)
        return jnp.concatenate([jnp.where(lane_lo, zz, zero), jnp.where(lane_lo, zero, zz)], axis=0)

    n_pairs = RW_W // LANES
    n_ch = min(WKV_CHUNKS_PER_STEP, tb // L)

    def phase1(gidx, chains):
        for q in range(n_ch):
            t0 = (gidx * n_ch + q) * L
            rows = slice(t0, t0 + L)
            lwc = lw_s[rows, :]
            l1, l2 = _split2(lwc)
            c = _dot(ltri, l1) + _dot(ltri, l2)
            c_last = c[L - 1:L, :]
            e_in = jnp.exp(c)
            e_prev = jnp.exp(c - lwc)
            e_out = jnp.exp(-c)
            g_end = jnp.exp(c_last)
            e_end = g_end * e_out
            kc = k_s[rows, :]
            vc = _bf(v_s[rows, :])
            betac = beta_s[rows, :]
            rt = _bf(r_s[rows, :] * e_in)
            kt = _bf(kc * e_out)
            bt = _bf(betac * e_out)
            kapt = _bf(kap_s[rows, :] * e_prev)
            khat = kc * e_end
            bhat = betac * e_end
            for pr in range(n_pairs):
                sl = slice(LANES * pr, LANES * (pr + 1))
                kts = stack(kt[:, sl])
                bts = stack(bt[:, sl])
                chains.append(dict(
                    q=q, pr=pr, rows=rows, sl=sl, rt=rt[:, sl], kapt=kapt[:, sl], v=vc[:, sl],
                    kb=jnp.concatenate([bts, kts], axis=0), kapts=stack(kapt[:, sl]),
                    vs=stack(vc[:, sl]), khat=khat[:, sl], bhat=bhat[:, sl], g_end=g_end[:, sl]))
            yield
        for ch in chains:
            ma = _dot_nt(jnp.concatenate([ch['kapt'], ch['rt']], axis=0), ch['kb'])
            ch['m_ab'] = jnp.where(strict, ma[:L, :LANES], 0.0)
            ch['m_ak'] = _bf(jnp.where(strict, ma[:L, LANES:], 0.0))
            ch['a_qb'] = _bf(jnp.where(incl, ma[L:, :LANES], 0.0))
            ch['a_qk'] = _bf(jnp.where(incl, ma[L:, LANES:], 0.0))
        yield
        for ch in chains:
            ch['t'] = eye_sbs - ch['m_ab']
            ch['pw'] = _bf(ch['m_ab'])
        for ch in chains:
            ch['pw'] = _bf(_dot(ch['pw'], stack(ch['pw'])))
        yield
        for it in range(5):
            for ch in chains:
                ch['pws'] = stack(ch['pw'])
            if it < 4:
                for ch in chains:
                    both = _dot(jnp.concatenate([ch['pw'], _bf(ch['t'])], axis=0), ch['pws'])
                    ch['pw_next'] = both[:L]
                    ch['t'] = ch['t'] + both[L:]
            else:
                for ch in chains:
                    ch['t'] = ch['t'] + _dot(_bf(ch['t']), ch['pws'])
            if it == 0:
                for ch in chains:
                    mvy = _dot(jnp.concatenate([ch['m_ak'], ch['a_qk']], axis=0), ch['vs'])
                    ch['mvs'] = stack(_bf(mvy[:L]))
                    ch['y0'] = mvy[L:]
            if it == 1:
                for ch in chains:
                    ch['khat_t'] = _bf(ch['khat'].T)
                    ch['bhat_t'] = _bf(ch['bhat'].T)
                    ch['g_rows'] = jnp.broadcast_to(ch['g_end'], (LANES, LANES)).T
            if it == 2:
                for ch in chains:
                    ch['kv0'] = _dot(ch['khat_t'], ch['v'])
            if it < 4:
                for ch in chains:
                    ch['pw'] = _bf(ch['pw_next'])
            yield
        for ch in chains:
            wu = _dot(_bf(ch['t']), jnp.concatenate([ch['kapts'], ch['mvs']], axis=1))
            ch['u0'] = wu[:, LANES:]
            ch['wr'] = jnp.concatenate([_bf(wu[:, :LANES]), ch['rt']], axis=0)
        yield

    def phase2(chains, st):
        for q in range(n_ch):
            cq = [ch for ch in chains if ch['q'] == q]
            st_b = [_bf(s_) for s_ in st]
            ws = [_dot(ch['wr'], st_b[ch['pr']]) for ch in cq]
            u_b = [_bf(ws[ch['pr']][:L] + ch['u0']) for ch in cq]
            yield
            st[:] = [ch['g_rows'] * st[ch['pr']]
                     + jnp.where(same_head, ch['kv0'] - _dot(ch['bhat_t'], u_b[ch['pr']]), 0.0) for ch in cq]
            for ch in cq:
                y_s[ch['rows'], ch['sl']] = (ws[ch['pr']][L:] + ch['y0']
                                             - _dot(ch['a_qb'], stack(u_b[ch['pr']])))
            yield

    def stage_c(gidx):
        rows = slice(gidx * n_ch * L, (gidx + 1) * n_ch * L)
        y = y_s[rows, :]
        mean = segsum(y) * (1.0 / HEAD_DIM)
        yield
        d = y - mean
        var = segsum(d * d) * (1.0 / HEAD_DIM)
        yield
        yn = d * lax.rsqrt(var + LNX_EPS) * lng_ref[...] + lnb_ref[...]
        o_ref[0, rows, :] = _bf((yn + bonus_s[rows, :]) * g_s[rows, :])
        yield

    st = [st_s[pr] for pr in range(n_pairs)]
    chains_of = {}
    n_groups = tb // (L * n_ch)
    for step in range(n_groups + 2):
        gens = []
        if step < n_groups:
            chains_of[step] = []
            gens.append(phase1(step, chains_of[step]))
        if 0 <= step - 1 < n_groups:
            gens.append(phase2(chains_of.pop(step - 1), st))
        if 0 <= step - 2 < n_groups:
            gens.append(stage_c(step - 2))
        _interleave(gens, background.get(step, ()))
    for pr in range(n_pairs):
        st_s[pr] = st[pr]
    _interleave([g_ for gs_ in background.values() for g_ in gs_])


def _poolconv_init(ubuf, sbuf_a, sbuf_b, cbuf):
    for buf in (ubuf, sbuf_a, sbuf_b):
        buf[0:2 * POOL_MAXW, :] = jnp.zeros((2 * POOL_MAXW, POOL_W), F32)
    cbuf[0:SUBLANES, :] = jnp.zeros((SUBLANES, CONV_W), F32)


def _poolconv_stages(j, h, wpc_ref, poolw_ref, pscale_ref, convw_ref, ob_ref, oc_ref,
                     ubuf, sbuf_a, sbuf_b, cbuf):
    tb = h.shape[0]
    pad = POOL_MAXW
    lo = 2 * POOL_MAXW
    u = _dot(h, wpc_ref[:, 0:POOL_W])
    ubuf[lo:lo + tb, :] = u
    yield
    assert POOL_WINDOWS == tuple(2 ** (gi + 1) for gi in range(len(POOL_WINDOWS)))
    src = ubuf
    sums = []
    for gi, win in enumerate(POOL_WINDOWS):
        dst = sbuf_a if gi % 2 == 0 else sbuf_b
        cols = slice(POOL_GW * gi, POOL_W)
        shift = win // 2
        dst[pad:lo + tb, cols] = src[pad:lo + tb, cols] + src[pad - shift:lo + tb - shift, cols]
        sums.append(dst)
        src = dst
        yield
    t_glob = j * tb + _iota((tb, 1), 0)
    for gi, win in enumerate(POOL_WINDOWS):
        cs = slice(POOL_GW * gi, POOL_GW * (gi + 1))
        cnt = jnp.minimum(t_glob + 1, win).astype(F32)
        zc = sums[gi][lo:lo + tb, cs] / cnt - u[:, cs]
        zz = _dot(_bf(zc), poolw_ref[gi])
        ob_ref[0, :, cs] = _bf(zz * pscale_ref[:, cs])
        yield
    ubuf[pad:lo, :] = ubuf[tb + pad:tb + lo, :]

    cg = _dot(h, wpc_ref[:, POOL_W + CONV_W:POOL_W + 2 * CONV_W])
    yield
    cu = _dot(h, wpc_ref[:, POOL_W + 2 * CONV_W:POOL_W + 3 * CONV_W])
    vv = cg * cu
    cbuf[SUBLANES:SUBLANES + tb, :] = vv
    yield
    bg = _dot(h, wpc_ref[:, POOL_W:POOL_W + CONV_W])
    conv = (convw_ref[0:1, :] * cbuf[SUBLANES - 2:SUBLANES - 2 + tb, :]
            + convw_ref[1:2, :] * cbuf[SUBLANES - 1:SUBLANES - 1 + tb, :]
            + convw_ref[2:3, :] * vv)
    cbuf[0:SUBLANES, :] = cbuf[tb:tb + SUBLANES, :]
    oc_ref[0] = _bf(bg * conv)
    yield


def _attn_init(sink_ref, kbuf0, kbuf1, vbuf, bias_tab):
    blk = ATT_BLOCK
    rows = ATT_G * blk
    kbuf0[0:blk, :] = jnp.zeros((blk, LANES), BF16)
    kbuf1[0:blk, :] = jnp.zeros((blk, LANES), BF16)
    vbuf[0:blk, :] = jnp.zeros((blk, LANES), BF16)
    qi = _iota((rows, 2 * blk), 0) % blk + blk
    kj = _iota((rows, 2 * blk), 1)
    dist = qi - kj
    band = (dist >= 0) & (dist < ATT_BLOCK)
    for g in range(ATT_HKV):
        sink = sink_ref[g]
        bias_tab[0, g] = jnp.where(kj == 0, sink, jnp.where(band & (kj >= blk), 0.0, -jnp.inf))
        bias_tab[1, g] = jnp.where(kj == 0, sink, jnp.where(band, 0.0, -jnp.inf))


def _attn_stages(j, h, watt_ref, qg_ref, kg_ref, cos_ref, sin_ref, o_ref, kbuf0, kbuf1, vbuf, bias_tab):
    tq = h.shape[0]
    blk = ATT_BLOCK
    nq = ATT_HQ * ATT_HD // LANES
    rows = nq * blk

    hi_ = (_iota((LANES, LANES), 0) % ATT_HD) // (ATT_HD // 2)
    hj_ = (_iota((LANES, LANES), 1) % ATT_HD) // (ATT_HD // 2)
    seg_mean = jnp.where(hi_ == hj_, 1.0 / ATT_HD, 0.0).astype(BF16)

    p = _dot(h, watt_ref[...])
    yield
    cos = cos_ref[0]
    sin = sin_ref[0]

    def norm_rope(xb, gain):
        ms = _dot(_bf(xb * xb), seg_mean)
        yb = xb * lax.rsqrt(ms + NORM_EPS) * gain
        return yb * cos + pltpu.roll(yb, LANES // 2, axis=1) * sin

    kn = norm_rope(p[:, nq * LANES:(nq + 1) * LANES], kg_ref[...])
    kv_lane = (_iota((tq, LANES), 1) % ATT_HD) // (ATT_HD // 2)
    kbuf0[blk:blk + tq, :] = _bf(jnp.where(kv_lane == 0, kn, 0.0))
    kbuf1[blk:blk + tq, :] = _bf(jnp.where(kv_lane == 1, kn, 0.0))
    vbuf[blk:blk + tq, :] = _bf(p[:, (nq + 1) * LANES:(nq + 2) * LANES])
    yield
    qs = [_bf(norm_rope(p[:, LANES * b:LANES * (b + 1)], qg_ref[...]) * (ATT_HD ** -0.5))
          for b in range(nq)]
    yield

    out_lo = _iota((rows, LANES), 1) < ATT_HD
    not_sink_row = _iota((2 * blk, LANES), 0) > 0
    ones_cols = jnp.ones((2 * blk, LANES), BF16)
    zero_kv = jnp.zeros((2 * blk, LANES), BF16)
    first_tab = jnp.where(j == 0, 0, 1)
    nblk = tq // blk
    nstep = min(ATT_BLOCKS_PER_STEP, nblk)
    for n0 in range(0, nblk, nstep):
        chains = []
        for n in range(n0, n0 + nstep):
            q_st = jnp.concatenate([q[blk * n:blk * (n + 1), :] for q in qs], axis=0)
            win = slice(blk * n, blk * (n + 2))
            v1 = jnp.concatenate([jnp.where(not_sink_row, vbuf[win, :], zero_kv), ones_cols], axis=1)
            for g, kbuf in enumerate((kbuf0, kbuf1)):
                chains.append(dict(n=n, g=g, q=q_st, v1=v1,
                                   kb=jnp.where(not_sink_row, kbuf[win, :], zero_kv)))
        for ch in chains:
            bias = bias_tab[first_tab, ch['g']] if ch['n'] == 0 else bias_tab[1, ch['g']]
            ch['sc'] = _dot_nt(ch['q'], ch['kb']) + bias
        yield
        for ch in chains:
            ch['m'] = jnp.max(ch['sc'], axis=-1, keepdims=True)
        yield
        for ch in chains:
            ch['e'] = _bf(jnp.exp(ch['sc'] - ch['m']))
        yield
        for ch in chains:
            ch['o'] = _dot(ch['e'], ch['v1'])
        yield
        for n in range(n0, n0 + nstep):
            c0, c1 = [ch for ch in chains if ch['n'] == n]
            num = jnp.where(out_lo, c0['o'][:, :LANES], c1['o'][:, :LANES])
            den = jnp.where(out_lo, c0['o'][:, LANES:], c1['o'][:, LANES:])
            o = num / den
            for b in range(nq):
                o_ref[0, blk * n:blk * (n + 1), LANES * b:LANES * (b + 1)] = _bf(o[blk * b:blk * (b + 1), :])
    for buf in (kbuf0, kbuf1, vbuf):
        buf[0:blk, :] = buf[tq:tq + blk, :]


def _interleave(gens, background=()):
    while gens:
        gens = [g_ for g_ in gens if next(g_, 'done') != 'done']
        for g_ in background:
            next(g_, 'done')


N_RWKV_IN = 12
N_RWKV_SCRATCH = 11


def _mixers_kernel(x_ref, g1_ref, *refs):
    rw_in, refs = refs[:N_RWKV_IN], refs[N_RWKV_IN:]
    (watt_ref, qg_ref, kg_ref, cos_ref, sin_ref, sink_ref, wpc_ref, poolw_ref, pscale_ref,
     convw_ref, oa_ref, od_ref, ob_ref, oc_ref) = refs[:14]
    rw_scr = refs[14:14 + N_RWKV_SCRATCH]
    kbuf0, kbuf1, vbuf, bias_tab, ubuf, sbuf_a, sbuf_b, cbuf = refs[14 + N_RWKV_SCRATCH:]
    j = pl.program_id(1)

    @pl.when(j == 0)
    def _():
        _rwkv_init(rw_scr[0], rw_scr[-1])
        _attn_init(sink_ref, kbuf0, kbuf1, vbuf, bias_tab)
        _poolconv_init(ubuf, sbuf_a, sbuf_b, cbuf)

    h = _bf(_rmsnorm(x_ref[0], g1_ref[...]))
    attn = _attn_stages(j, h, watt_ref, qg_ref, kg_ref, cos_ref, sin_ref, od_ref, kbuf0, kbuf1, vbuf, bias_tab)
    poolconv = _poolconv_stages(j, h, wpc_ref, poolw_ref, pscale_ref, convw_ref, ob_ref, oc_ref,
                                ubuf, sbuf_a, sbuf_b, cbuf)
    _rwkv_body(h, *rw_in, oa_ref, *rw_scr, background={1: [attn], 2: [poolconv]})


def _mixers_call(x, g1, rw_in, watt, qg, kg, cos, sin, sink_rows, wpc, poolw, pscale, convw, ts):
    b, s, _ = x.shape
    const = lambda shape: pl.BlockSpec(shape, lambda i, j: (0,) * len(shape))
    tile = lambda w: pl.BlockSpec((1, ts, w), lambda i, j: (i, j, 0))
    out = lambda w: jax.ShapeDtypeStruct((b, s, w), BF16)
    row = const((1, RW_W))
    scr = lambda: pltpu.VMEM((ts, RW_W), F32)
    assert len(rw_in) == N_RWKV_IN
    return pl.pallas_call(
        _mixers_kernel,
        out_shape=(out(RW_W), out(ATT_HQ * ATT_HD), out(POOL_W), out(CONV_W)),
        grid=(b, s // ts),
        in_specs=[tile(D_MODEL), const((1, D_MODEL)),
                  const((D_MODEL, RW_COLS)), const((1, RW_COLS)),
                  const((LANES, RW_W)), const((LANES, RW_W)), const((GATE_LORA, RW_W)),
                  row, row, row, row, row, row, row,
                  const((D_MODEL, ATT_COLS)),
                  const((1, LANES)), const((1, LANES)), tile(LANES), tile(LANES),
                  const((ATT_HKV, ATT_G * ATT_BLOCK, 1)),
                  const((D_MODEL, POOL_W + 3 * CONV_W)),
                  const((len(POOL_WINDOWS), POOL_GW, POOL_GW)), const((1, POOL_W)),
                  const((CONV_K, CONV_W))],
        out_specs=(tile(RW_W), tile(ATT_HQ * ATT_HD), tile(POOL_W), tile(CONV_W)),
        scratch_shapes=[pltpu.VMEM((ts + SUBLANES, RW_COLS), F32),
                        scr(), scr(), scr(), scr(), scr(), scr(), scr(), scr(), scr(),
                        pltpu.VMEM((RW_W // LANES, LANES, LANES), F32),
                        pltpu.VMEM((ts + ATT_BLOCK, LANES), BF16),
                        pltpu.VMEM((ts + ATT_BLOCK, LANES), BF16),
                        pltpu.VMEM((ts + ATT_BLOCK, LANES), BF16),
                        pltpu.VMEM((2, ATT_HKV, ATT_G * ATT_BLOCK, 2 * ATT_BLOCK), F32),
                        pltpu.VMEM((ts + 2 * POOL_MAXW, POOL_W), F32),
                        pltpu.VMEM((ts + 2 * POOL_MAXW, POOL_W), F32),
                        pltpu.VMEM((ts + 2 * POOL_MAXW, POOL_W), F32),
                        pltpu.VMEM((ts + SUBLANES, CONV_W), F32)],
        compiler_params=pltpu.CompilerParams(dimension_semantics=("arbitrary", "arbitrary"),
                                             vmem_limit_bytes=VMEM_LIMIT),
        name="token_mixers",
    )(x, g1, *rw_in, watt, qg, kg, cos, sin, sink_rows, wpc, poolw, pscale, convw)


def _merge_kernel(x_ref, g1_ref, wg_ref, za_ref, zb_ref, zc_ref, zd_ref,
                  wa_ref, wb_ref, wc_ref, wd_ref, wo_ref, o_ref):
    x = x_ref[...]
    h = _bf(_rmsnorm(x, g1_ref[...]))
    mixed = None
    for b, (z_ref, w_ref) in enumerate(((za_ref, wa_ref), (zb_ref, wb_ref), (zc_ref, wc_ref), (zd_ref, wd_ref))):
        gate = jax.nn.sigmoid(_dot(h, wg_ref[:, D_MODEL * b:D_MODEL * (b + 1)]))
        term = gate * _dot(z_ref[...], w_ref[...])
        mixed = term if mixed is None else mixed + term
    o_ref[...] = x + _dot(_bf(mixed), wo_ref[...])


def _merge_call(x2, g1, wg, za, zb, zc, zd, wa, wb, wc, wd, wo, tm):
    t = x2.shape[0]
    const = lambda shape: pl.BlockSpec(shape, lambda i: (0,) * len(shape))
    tile = lambda w: pl.BlockSpec((tm, w), lambda i: (i, 0))
    wout = const((RW_W, D_MODEL))
    return pl.pallas_call(
        _merge_kernel,
        out_shape=jax.ShapeDtypeStruct((t, D_MODEL), F32),
        grid=(t // tm,),
        in_specs=[tile(D_MODEL), const((1, D_MODEL)), const((D_MODEL, N_BRANCH * D_MODEL)),
                  tile(RW_W), tile(POOL_W), tile(CONV_W), tile(ATT_HQ * ATT_HD),
                  wout, wout, wout, wout, const((D_MODEL, D_MODEL))],
        out_specs=tile(D_MODEL),
        compiler_params=pltpu.CompilerParams(dimension_semantics=("arbitrary",),
                                             vmem_limit_bytes=VMEM_LIMIT),
        name="merge_mixers",
    )(x2, g1, wg, za, zb, zc, zd, wa, wb, wc, wd, wo)


def _ffn_kernel(x_ref, g2_ref, wg_ref, wu_ref, wd_ref, o_ref):
    x = x_ref[...]
    h = _bf(_rmsnorm(x, g2_ref[...]))
    acc = x
    for c in range(D_FF // FFN_CHUNK):
        cs = slice(FFN_CHUNK * c, FFN_CHUNK * (c + 1))
        gt = _dot(h, wg_ref[:, cs])
        up = _dot(h, wu_ref[:, cs])
        act = _bf(gt * jax.nn.sigmoid(gt) * up)
        acc = acc + _dot(act, wd_ref[cs, :])
    o_ref[...] = acc


def _ffn_call(x2, g2, wg, wu, wd, tm):
    t = x2.shape[0]
    const = lambda shape: pl.BlockSpec(shape, lambda i: (0,) * len(shape))
    tile = pl.BlockSpec((tm, D_MODEL), lambda i: (i, 0))
    return pl.pallas_call(
        _ffn_kernel,
        out_shape=jax.ShapeDtypeStruct((t, D_MODEL), F32),
        grid=(t // tm,),
        in_specs=[tile, const((1, D_MODEL)), const((D_MODEL, D_FF)), const((D_MODEL, D_FF)),
                  const((D_FF, D_MODEL))],
        out_specs=tile,
        compiler_params=pltpu.CompilerParams(dimension_semantics=("arbitrary",),
                                             vmem_limit_bytes=VMEM_LIMIT),
        name="ffn_swiglu",
    )(x2, g2, wg, wu, wd)


def _attn_perms():
    half = ATT_HD // 2
    q_cols = []
    for jb in range(ATT_G):
        for hf in range(2):
            for ab in range(ATT_HKV):
                head = ATT_G * ab + jb
                q_cols += [ATT_HD * head + half * hf + i for i in range(half)]
    k_cols = []
    for hf in range(2):
        for g in range(ATT_HKV):
            k_cols += [ATT_HD * g + half * hf + i for i in range(half)]
    gain_idx = [half * ((l % LANES) // (LANES // 2)) + l % half for l in range(LANES)]
    o_rows = []
    for jb in range(ATT_G):
        for ab in range(ATT_HKV):
            head = ATT_G * ab + jb
            o_rows += [ATT_HD * head + c for c in range(ATT_HD)]
    return q_cols, k_cols, gain_idx, o_rows


def kernel(x, positions, norm1_g, w_in, shift_mu, w_decay_up, w0, a_up, a0, g_up, k_k, k_a, r_k, lnx_g, lnx_b, w_rwkv_out, pool_w, pool_scale, w_pool_out, conv_w, w_conv_out, q_norm_g, k_norm_g, sinks, w_attn_out, w_o, norm2_g, w_ffn_gate, w_ffn_up, w_ffn_down):
    b, s, d = x.shape
    assert d == D_MODEL and s % ATT_BLOCK == 0
    ts = min(SEQ_TILE, s)
    tm = min(TOK_TILE, b * s)
    assert s % ts == 0 and (b * s) % tm == 0
    depth = w_in.shape[0]

    q_cols, k_cols, gain_idx, o_rows = _attn_perms()
    q_cols = jnp.asarray(q_cols, jnp.int32)
    k_cols = jnp.asarray(k_cols, jnp.int32)
    gain_idx = jnp.asarray(gain_idx, jnp.int32)
    o_rows = jnp.asarray(o_rows, jnp.int32)
    c_rw, c_pool, c_conv, c_att = RW_COLS, RW_COLS + POOL_W, RW_COLS + POOL_W + 3 * CONV_W, \
        RW_COLS + POOL_W + 3 * CONV_W + ATT_COLS

    cos, sin = _rope_tables(positions, ts)
    row = lambda v: v.reshape(1, -1).astype(F32)
    zeros_lora = jnp.zeros((DECAY_LORA, RW_W), F32)

    for i in range(depth):
        g1 = row(norm1_g[i])
        wi = w_in[i]
        wdp = _bf(jnp.concatenate([w_decay_up[i], zeros_lora], axis=0))
        wap = _bf(jnp.concatenate([zeros_lora, a_up[i]], axis=0))
        rw_in = (_bf(wi[:, :c_rw]), row(shift_mu[i]), wdp, wap, _bf(g_up[i]),
                 row(w0[i]), row(a0[i]), row(k_k[i]), row(k_a[i]), row(r_k[i]),
                 row(lnx_g[i]), row(lnx_b[i]))
        w_att = wi[:, c_conv:c_att]
        w_att = jnp.concatenate([w_att[:, q_cols], w_att[:, ATT_HQ * ATT_HD + k_cols],
                                 w_att[:, (ATT_HQ + ATT_HKV) * ATT_HD:]], axis=1)
        sink_rows = jnp.repeat(sinks[i].astype(F32).reshape(ATT_HKV, ATT_G), ATT_BLOCK, axis=1)[..., None]
        za, zd, zb, zc = _mixers_call(
            x, g1, rw_in, _bf(w_att), row(q_norm_g[i][gain_idx]), row(k_norm_g[i][gain_idx]),
            cos, sin, sink_rows, _bf(wi[:, c_rw:c_conv]), _bf(pool_w[i]),
            row(pool_scale[i]), conv_w[i].astype(F32), ts)
        flat = lambda z: z.reshape(b * s, z.shape[-1])
        x1 = _merge_call(flat(x), g1, _bf(wi[:, c_att:]), flat(za), flat(zb), flat(zc), flat(zd),
                         _bf(w_rwkv_out[i]), _bf(w_pool_out[i]), _bf(w_conv_out[i]),
                         _bf(w_attn_out[i][o_rows, :]), _bf(w_o[i]), tm)
        x2 = _ffn_call(x1, row(norm2_g[i]), _bf(w_ffn_gate[i]), _bf(w_ffn_up[i]), _bf(w_ffn_down[i]), tm)
        x = x2.reshape(b, s, d)
    return x
```

```python
import math

import jax
import jax.numpy as jnp
from jax import lax
from jax.experimental import pallas as pl
from jax.experimental.pallas import tpu as pltpu

F32 = jnp.float32
BF16 = jnp.bfloat16

D_MODEL = 1024
RW_HEADS = 8
HEAD_DIM = 64
RW_W = RW_HEADS * HEAD_DIM
DECAY_LORA = 64
ICLR_LORA = 64
GATE_LORA = 128
LNX_EPS = 64e-5
RW_COLS = 3 * RW_W + DECAY_LORA + ICLR_LORA + GATE_LORA
POOL_W = 512
POOL_GW = 128
POOL_WINDOWS = (2, 4, 8, 16)
POOL_MAXW = 16
CONV_W = 512
CONV_K = 3
ATT_HQ = 8
ATT_HKV = 2
ATT_G = ATT_HQ // ATT_HKV
ATT_HD = 64
ATT_BLOCK = 128
ATT_COLS = (ATT_HQ + 2 * ATT_HKV) * ATT_HD
ROPE_THETA = 10000.0
N_BRANCH = 4
D_FF = 2816
NORM_EPS = 1e-6

LANES = 128
SUBLANES = 8
WKV_CHUNK = 64
WKV_CHUNKS_PER_STEP = 4
ATT_BLOCKS_PER_STEP = 2
SEQ_TILE = 512
TOK_TILE = 1024
FFN_CHUNK = 256
VMEM_LIMIT = 48 * 1024 * 1024


def _bf(x):
    return x.astype(BF16)


def _dot(a, b):
    return jnp.dot(a, b, preferred_element_type=F32)


def _dot_nt(a, b):
    return lax.dot_general(a, b, (((1,), (1,)), ((), ())), preferred_element_type=F32)


def _split2(x):
    hi = _bf(x)
    lo = _bf(x - hi.astype(F32))
    return hi, lo


def _rmsnorm(x, g):
    ms = jnp.mean(x * x, axis=-1, keepdims=True)
    return x * lax.rsqrt(ms + NORM_EPS) * g


def _iota(shape, dim):
    return lax.broadcasted_iota(jnp.int32, shape, dim)


def _rope_kernel(pos_ref, inv_ref, cos_ref, sin_ref):
    ang = pos_ref[0].astype(F32) * inv_ref[...]
    lane = _iota(ang.shape, 1)
    cos_ref[0] = jnp.cos(ang)
    s = jnp.sin(ang)
    sin_ref[0] = jnp.where(lane < LANES // 2, -s, s)


def _rope_tables(positions, ts):
    b, s = positions.shape
    half = ATT_HD // 2
    inv = ROPE_THETA ** (-jnp.arange(half, dtype=F32) * 2.0 / ATT_HD)
    inv = jnp.tile(inv, LANES // half)[None, :]
    pos3 = positions.reshape(b, s, 1)
    out = jax.ShapeDtypeStruct((b, s, LANES), F32)
    return pl.pallas_call(
        _rope_kernel,
        out_shape=(out, out),
        grid=(b, s // ts),
        in_specs=[pl.BlockSpec((1, ts, 1), lambda i, j: (i, j, 0)),
                  pl.BlockSpec((1, LANES), lambda i, j: (0, 0))],
        out_specs=(pl.BlockSpec((1, ts, LANES), lambda i, j: (i, j, 0)),
                   pl.BlockSpec((1, ts, LANES), lambda i, j: (i, j, 0))),
        compiler_params=pltpu.CompilerParams(dimension_semantics=("arbitrary", "arbitrary")),
        name="rope_tables",
    )(pos3, inv)


def _rwkv_init(pbuf, st_s):
    st_s[...] = jnp.zeros(st_s.shape, F32)
    pbuf[0:SUBLANES, :] = jnp.zeros((SUBLANES, RW_COLS), F32)


def _rwkv_body(h, wrw_ref, mu_ref, wdp_ref, wap_ref, gup_ref, w0_ref, a0_ref,
               kk_ref, ka_ref, rk_ref, lng_ref, lnb_ref, o_ref,
               pbuf, r_s, k_s, v_s, kap_s, beta_s, lw_s, bonus_s, g_s, y_s, st_s, background):
    tb = h.shape[0]
    L = WKV_CHUNK

    gi = _iota((LANES, LANES), 0) // HEAD_DIM
    gj = _iota((LANES, LANES), 1) // HEAD_DIM
    seg_ones = jnp.where(gi == gj, 1.0, 0.0).astype(BF16)

    def segsum(z):
        return jnp.concatenate(
            [_dot(_bf(z[:, LANES * b:LANES * (b + 1)]), seg_ones) for b in range(RW_W // LANES)], axis=1)

    p = _dot(h, wrw_ref[...])
    pbuf[SUBLANES:SUBLANES + tb, :] = p
    p_prev = pbuf[SUBLANES - 1:SUBLANES - 1 + tb, :]
    pbuf[SUBLANES - 1:SUBLANES, :] = p[tb - 1:tb, :]
    pm = p + (p_prev - p) * mu_ref[...]
    r = pm[:, 0:RW_W]
    k = pm[:, RW_W:2 * RW_W]
    v = pm[:, 2 * RW_W:3 * RW_W]
    lora_in = pm[:, 3 * RW_W:3 * RW_W + LANES]
    gd = pm[:, 3 * RW_W + LANES:RW_COLS]
    z = w0_ref[...] + _dot(_bf(jnp.tanh(lora_in)), wdp_ref[...])
    lw_s[...] = (-math.exp(-0.5)) * jax.nn.sigmoid(z)
    a = jax.nn.sigmoid(a0_ref[...] + _dot(_bf(lora_in), wap_ref[...]))
    g_s[...] = _dot(_bf(jax.nn.sigmoid(gd)), gup_ref[...])
    kk = k * kk_ref[...]
    kap = kk * lax.rsqrt(jnp.maximum(segsum(kk * kk), 1e-24))
    k2 = k * (1.0 + (a - 1.0) * ka_ref[...])
    r_s[...] = r
    k_s[...] = k2
    v_s[...] = v
    kap_s[...] = kap
    beta_s[...] = kap * a
    bonus_s[...] = segsum(r * k2 * rk_ref[...]) * v

    assert L == HEAD_DIM
    lane_lo = _iota((L, LANES), 1) < HEAD_DIM
    tok = _iota((L, LANES), 0)
    col = _iota((L, LANES), 1) % L
    strict = col < tok
    incl = col <= tok
    eye_sbs = jnp.where(col == tok, 1.0, 0.0)
    same_head = (_iota((LANES, LANES), 0) // HEAD_DIM) == (_iota((LANES, LANES), 1) // HEAD_DIM)
    ltri2 = jnp.where(_iota((L, 2 * L), 1) % L <= _iota((L, 2 * L), 0), 1.0, 0.0).astype(BF16)

    def stack(zz):
        zero = jnp.zeros_like(zz)
        return jnp.concatenate([jnp.where(lane_lo, zz, zero), jnp.where(lane_lo, zero, zz)], axis=0)

    n_pairs = RW_W // LANES
    n_ch = min(WKV_CHUNKS_PER_STEP, tb // L)

    def phase1(gidx, chains):
        for q in range(n_ch):
            t0 = (gidx * n_ch + q) * L
            rows = slice(t0, t0 + L)
            lwc = lw_s[rows, :]
            c = _dot(ltri2, jnp.concatenate(_split2(lwc), axis=0))
            c_last = c[L - 1:L, :]
            e_in = jnp.exp(c)
            e_prev = jnp.exp(c - lwc)
            e_out = jnp.exp(-c)
            g_end = jnp.exp(c_last)
            e_end = g_end * e_out
            kc = k_s[rows, :]
            vc = _bf(v_s[rows, :])
            betac = beta_s[rows, :]
            rt = _bf(r_s[rows, :] * e_in)
            kt = _bf(kc * e_out)
            bt = _bf(betac * e_out)
            kapt = _bf(kap_s[rows, :] * e_prev)
            khat = kc * e_end
            bhat = betac * e_end
            for pr in range(n_pairs):
                sl = slice(LANES * pr, LANES * (pr + 1))
                kts = stack(kt[:, sl])
                bts = stack(bt[:, sl])
                chains.append(dict(
                    q=q, pr=pr, rows=rows, sl=sl, rt=rt[:, sl], kapt=kapt[:, sl], v=vc[:, sl],
                    kb=jnp.concatenate([bts, kts], axis=0), kapts=stack(kapt[:, sl]),
                    vs=stack(vc[:, sl]), khat=khat[:, sl], bhat=bhat[:, sl], g_end=g_end[:, sl]))
            yield
        for ch in chains:
            ma = _dot_nt(jnp.concatenate([ch['kapt'], ch['rt']], axis=0), ch['kb'])
            ch['m_ab'] = jnp.where(strict, ma[:L, :LANES], 0.0)
            ch['m_ak'] = _bf(jnp.where(strict, ma[:L, LANES:], 0.0))
            ch['a_qb'] = _bf(jnp.where(incl, ma[L:, :LANES], 0.0))
            ch['a_qk'] = _bf(jnp.where(incl, ma[L:, LANES:], 0.0))
        yield
        for ch in chains:
            ch['t'] = eye_sbs - ch['m_ab']
            ch['pw'] = _bf(ch['m_ab'])
        for ch in chains:
            ch['pw'] = _bf(_dot(ch['pw'], stack(ch['pw'])))
        yield
        for it in range(5):
            for ch in chains:
                ch['pws'] = stack(ch['pw'])
            if it < 4:
                for ch in chains:
                    both = _dot(jnp.concatenate([ch['pw'], _bf(ch['t'])], axis=0), ch['pws'])
                    ch['pw_next'] = both[:L]
                    ch['t'] = ch['t'] + both[L:]
            else:
                for ch in chains:
                    ch['t'] = ch['t'] + _dot(_bf(ch['t']), ch['pws'])
            if it == 0:
                for ch in chains:
                    mvy = _dot(jnp.concatenate([ch['m_ak'], ch['a_qk']], axis=0), ch['vs'])
                    ch['mvs'] = stack(_bf(mvy[:L]))
                    ch['y0'] = mvy[L:]
            if it == 1:
                for ch in chains:
                    ch['kb_t'] = jnp.concatenate([_bf(ch['khat'].T), _bf(ch['bhat'].T)], axis=1)
                    ch['g_rows'] = jnp.broadcast_to(ch['g_end'], (LANES, LANES)).T
            if it < 4:
                for ch in chains:
                    ch['pw'] = _bf(ch['pw_next'])
            yield
        for ch in chains:
            wu = _dot(_bf(ch['t']), jnp.concatenate([ch['kapts'], ch['mvs']], axis=1))
            ch['u0'] = wu[:, LANES:]
            ch['wr'] = jnp.concatenate([_bf(wu[:, :LANES]), ch['rt']], axis=0)
        yield

    def phase2(chains, st):
        for q in range(n_ch):
            cq = [ch for ch in chains if ch['q'] == q]
            st_b = [_bf(s_) for s_ in st]
            ws = [_dot(ch['wr'], st_b[ch['pr']]) for ch in cq]
            u_b = [_bf(ws[ch['pr']][:L] + ch['u0']) for ch in cq]
            yield
            st[:] = [ch['g_rows'] * st[ch['pr']]
                     + jnp.where(same_head, _dot(ch['kb_t'], jnp.concatenate([ch['v'], -u_b[ch['pr']]], axis=0)), 0.0)
                     for ch in cq]
            for ch in cq:
                y_s[ch['rows'], ch['sl']] = (ws[ch['pr']][L:] + ch['y0']
                                             - _dot(ch['a_qb'], stack(u_b[ch['pr']])))
            yield

    def stage_c(gidx):
        rows = slice(gidx * n_ch * L, (gidx + 1) * n_ch * L)
        y = y_s[rows, :]
        mean = segsum(y) * (1.0 / HEAD_DIM)
        yield
        d = y - mean
        var = segsum(d * d) * (1.0 / HEAD_DIM)
        yield
        yn = d * lax.rsqrt(var + LNX_EPS) * lng_ref[...] + lnb_ref[...]
        o_ref[0, rows, :] = _bf((yn + bonus_s[rows, :]) * g_s[rows, :])
        yield

    st = [st_s[pr] for pr in range(n_pairs)]
    chains_of = {}
    n_groups = tb // (L * n_ch)
    for step in range(n_groups + 2):
        gens = []
        if step < n_groups:
            chains_of[step] = []
            gens.append(phase1(step, chains_of[step]))
        if 0 <= step - 1 < n_groups:
            gens.append(phase2(chains_of.pop(step - 1), st))
        if 0 <= step - 2 < n_groups:
            gens.append(stage_c(step - 2))
        _interleave(gens, background.get(step, ()))
    for pr in range(n_pairs):
        st_s[pr] = st[pr]
    _interleave([g_ for gs_ in background.values() for g_ in gs_])


def _poolconv_init(ubuf, sbuf_a, sbuf_b, cbuf):
    for buf in (ubuf, sbuf_a, sbuf_b):
        buf[0:2 * POOL_MAXW, :] = jnp.zeros((2 * POOL_MAXW, POOL_W), F32)
    cbuf[0:SUBLANES, :] = jnp.zeros((SUBLANES, CONV_W), F32)


def _poolconv_stages(j, h, wpc_ref, poolw_ref, pscale_ref, convw_ref, ob_ref, oc_ref,
                     ubuf, sbuf_a, sbuf_b, cbuf):
    tb = h.shape[0]
    pad = POOL_MAXW
    lo = 2 * POOL_MAXW
    u = _dot(h, wpc_ref[:, 0:POOL_W])
    ubuf[lo:lo + tb, :] = u
    yield
    assert POOL_WINDOWS == tuple(2 ** (gi + 1) for gi in range(len(POOL_WINDOWS)))
    src = ubuf
    sums = []
    for gi, win in enumerate(POOL_WINDOWS):
        dst = sbuf_a if gi % 2 == 0 else sbuf_b
        cols = slice(POOL_GW * gi, POOL_W)
        shift = win // 2
        dst[pad:lo + tb, cols] = src[pad:lo + tb, cols] + src[pad - shift:lo + tb - shift, cols]
        sums.append(dst)
        src = dst
        yield
    t_glob = j * tb + _iota((tb, 1), 0)
    for gi, win in enumerate(POOL_WINDOWS):
        cs = slice(POOL_GW * gi, POOL_GW * (gi + 1))
        cnt = jnp.minimum(t_glob + 1, win).astype(F32)
        zc = sums[gi][lo:lo + tb, cs] / cnt - u[:, cs]
        zz = _dot(_bf(zc), poolw_ref[gi])
        ob_ref[0, :, cs] = _bf(zz * pscale_ref[:, cs])
        yield
    ubuf[pad:lo, :] = ubuf[tb + pad:tb + lo, :]

    cg = _dot(h, wpc_ref[:, POOL_W + CONV_W:POOL_W + 2 * CONV_W])
    yield
    cu = _dot(h, wpc_ref[:, POOL_W + 2 * CONV_W:POOL_W + 3 * CONV_W])
    vv = cg * cu
    cbuf[SUBLANES:SUBLANES + tb, :] = vv
    yield
    bg = _dot(h, wpc_ref[:, POOL_W:POOL_W + CONV_W])
    conv = (convw_ref[0:1, :] * cbuf[SUBLANES - 2:SUBLANES - 2 + tb, :]
            + convw_ref[1:2, :] * cbuf[SUBLANES - 1:SUBLANES - 1 + tb, :]
            + convw_ref[2:3, :] * vv)
    cbuf[0:SUBLANES, :] = cbuf[tb:tb + SUBLANES, :]
    oc_ref[0] = _bf(bg * conv)
    yield


def _attn_init(sink_ref, kbuf0, kbuf1, vbuf, bias_tab):
    blk = ATT_BLOCK
    rows = ATT_G * blk
    kbuf0[0:blk, :] = jnp.zeros((blk, LANES), BF16)
    kbuf1[0:blk, :] = jnp.zeros((blk, LANES), BF16)
    vbuf[0:blk, :] = jnp.zeros((blk, LANES), BF16)
    qi = _iota((rows, 2 * blk), 0) % blk + blk
    kj = _iota((rows, 2 * blk), 1)
    dist = qi - kj
    band = (dist >= 0) & (dist < ATT_BLOCK)
    for g in range(ATT_HKV):
        sink = sink_ref[g]
        bias_tab[0, g] = jnp.where(kj == 0, sink, jnp.where(band & (kj >= blk), 0.0, -jnp.inf))
        bias_tab[1, g] = jnp.where(kj == 0, sink, jnp.where(band, 0.0, -jnp.inf))


def _attn_stages(j, h, watt_ref, qg_ref, kg_ref, cos_ref, sin_ref, o_ref, kbuf0, kbuf1, vbuf, bias_tab):
    tq = h.shape[0]
    blk = ATT_BLOCK
    nq = ATT_HQ * ATT_HD // LANES
    rows = nq * blk

    hi_ = (_iota((LANES, LANES), 0) % ATT_HD) // (ATT_HD // 2)
    hj_ = (_iota((LANES, LANES), 1) % ATT_HD) // (ATT_HD // 2)
    seg_mean = jnp.where(hi_ == hj_, 1.0 / ATT_HD, 0.0).astype(BF16)

    p = _dot(h, watt_ref[...])
    yield
    cos = cos_ref[0]
    sin = sin_ref[0]

    def norm_rope(xb, gain):
        ms = _dot(_bf(xb * xb), seg_mean)
        yb = xb * lax.rsqrt(ms + NORM_EPS) * gain
        return yb * cos + pltpu.roll(yb, LANES // 2, axis=1) * sin

    kn = norm_rope(p[:, nq * LANES:(nq + 1) * LANES], kg_ref[...])
    kv_lane = (_iota((tq, LANES), 1) % ATT_HD) // (ATT_HD // 2)
    kbuf0[blk:blk + tq, :] = _bf(jnp.where(kv_lane == 0, kn, 0.0))
    kbuf1[blk:blk + tq, :] = _bf(jnp.where(kv_lane == 1, kn, 0.0))
    vbuf[blk:blk + tq, :] = _bf(p[:, (nq + 1) * LANES:(nq + 2) * LANES])
    yield
    qs = [_bf(norm_rope(p[:, LANES * b:LANES * (b + 1)], qg_ref[...]) * (ATT_HD ** -0.5))
          for b in range(nq)]
    yield

    out_lo = _iota((rows, LANES), 1) < ATT_HD
    not_sink_row = _iota((2 * blk, LANES), 0) > 0
    ones_cols = jnp.ones((2 * blk, LANES), BF16)
    zero_kv = jnp.zeros((2 * blk, LANES), BF16)
    first_tab = jnp.where(j == 0, 0, 1)
    nblk = tq // blk
    nstep = min(ATT_BLOCKS_PER_STEP, nblk)
    for n0 in range(0, nblk, nstep):
        chains = []
        for n in range(n0, n0 + nstep):
            q_st = jnp.concatenate([q[blk * n:blk * (n + 1), :] for q in qs], axis=0)
            win = slice(blk * n, blk * (n + 2))
            v1 = jnp.concatenate([jnp.where(not_sink_row, vbuf[win, :], zero_kv), ones_cols], axis=1)
            for g, kbuf in enumerate((kbuf0, kbuf1)):
                chains.append(dict(n=n, g=g, q=q_st, v1=v1,
                                   kb=jnp.where(not_sink_row, kbuf[win, :], zero_kv)))
        for ch in chains:
            bias = bias_tab[first_tab, ch['g']] if ch['n'] == 0 else bias_tab[1, ch['g']]
            ch['sc'] = _dot_nt(ch['q'], ch['kb']) + bias
        yield
        for ch in chains:
            ch['m'] = jnp.max(ch['sc'], axis=-1, keepdims=True)
        yield
        for ch in chains:
            ch['e'] = _bf(jnp.exp(ch['sc'] - ch['m']))
        yield
        for ch in chains:
            ch['o'] = _dot(ch['e'], ch['v1'])
        yield
        for n in range(n0, n0 + nstep):
            c0, c1 = [ch for ch in chains if ch['n'] == n]
            num = jnp.where(out_lo, c0['o'][:, :LANES], c1['o'][:, :LANES])
            den = jnp.where(out_lo, c0['o'][:, LANES:], c1['o'][:, LANES:])
            o = num / den
            for b in range(nq):
                o_ref[0, blk * n:blk * (n + 1), LANES * b:LANES * (b + 1)] = _bf(o[blk * b:blk * (b + 1), :])
    for buf in (kbuf0, kbuf1, vbuf):
        buf[0:blk, :] = buf[tq:tq + blk, :]


def _interleave(gens, background=()):
    while gens:
        gens = [g_ for g_ in gens if next(g_, 'done') != 'done']
        for g_ in background:
            next(g_, 'done')


N_RWKV_IN = 12
N_RWKV_SCRATCH = 11


def _mixers_kernel(x_ref, g1_ref, *refs):
    rw_in, refs = refs[:N_RWKV_IN], refs[N_RWKV_IN:]
    (watt_ref, qg_ref, kg_ref, cos_ref, sin_ref, sink_ref, wpc_ref, poolw_ref, pscale_ref,
     convw_ref, oa_ref, od_ref, ob_ref, oc_ref) = refs[:14]
    rw_scr = refs[14:14 + N_RWKV_SCRATCH]
    kbuf0, kbuf1, vbuf, bias_tab, ubuf, sbuf_a, sbuf_b, cbuf = refs[14 + N_RWKV_SCRATCH:]
    j = pl.program_id(1)

    @pl.when(j == 0)
    def _():
        _rwkv_init(rw_scr[0], rw_scr[-1])
        _attn_init(sink_ref, kbuf0, kbuf1, vbuf, bias_tab)
        _poolconv_init(ubuf, sbuf_a, sbuf_b, cbuf)

    h = _bf(_rmsnorm(x_ref[0], g1_ref[...]))
    attn = _attn_stages(j, h, watt_ref, qg_ref, kg_ref, cos_ref, sin_ref, od_ref, kbuf0, kbuf1, vbuf, bias_tab)
    poolconv = _poolconv_stages(j, h, wpc_ref, poolw_ref, pscale_ref, convw_ref, ob_ref, oc_ref,
                                ubuf, sbuf_a, sbuf_b, cbuf)
    _rwkv_body(h, *rw_in, oa_ref, *rw_scr, background={1: [attn], 2: [poolconv]})


def _mixers_call(x, g1, rw_in, watt, qg, kg, cos, sin, sink_rows, wpc, poolw, pscale, convw, ts):
    b, s, _ = x.shape
    const = lambda shape: pl.BlockSpec(shape, lambda i, j: (0,) * len(shape))
    tile = lambda w: pl.BlockSpec((1, ts, w), lambda i, j: (i, j, 0))
    out = lambda w: jax.ShapeDtypeStruct((b, s, w), BF16)
    row = const((1, RW_W))
    scr = lambda: pltpu.VMEM((ts, RW_W), F32)
    assert len(rw_in) == N_RWKV_IN
    return pl.pallas_call(
        _mixers_kernel,
        out_shape=(out(RW_W), out(ATT_HQ * ATT_HD), out(POOL_W), out(CONV_W)),
        grid=(b, s // ts),
        in_specs=[tile(D_MODEL), const((1, D_MODEL)),
                  const((D_MODEL, RW_COLS)), const((1, RW_COLS)),
                  const((LANES, RW_W)), const((LANES, RW_W)), const((GATE_LORA, RW_W)),
                  row, row, row, row, row, row, row,
                  const((D_MODEL, ATT_COLS)),
                  const((1, LANES)), const((1, LANES)), tile(LANES), tile(LANES),
                  const((ATT_HKV, ATT_G * ATT_BLOCK, 1)),
                  const((D_MODEL, POOL_W + 3 * CONV_W)),
                  const((len(POOL_WINDOWS), POOL_GW, POOL_GW)), const((1, POOL_W)),
                  const((CONV_K, CONV_W))],
        out_specs=(tile(RW_W), tile(ATT_HQ * ATT_HD), tile(POOL_W), tile(CONV_W)),
        scratch_shapes=[pltpu.VMEM((ts + SUBLANES, RW_COLS), F32),
                        scr(), scr(), scr(), scr(), scr(), scr(), scr(), scr(), scr(),
                        pltpu.VMEM((RW_W // LANES, LANES, LANES), F32),
                        pltpu.VMEM((ts + ATT_BLOCK, LANES), BF16),
                        pltpu.VMEM((ts + ATT_BLOCK, LANES), BF16),
                        pltpu.VMEM((ts + ATT_BLOCK, LANES), BF16),
                        pltpu.VMEM((2, ATT_HKV, ATT_G * ATT_BLOCK, 2 * ATT_BLOCK), F32),
                        pltpu.VMEM((ts + 2 * POOL_MAXW, POOL_W), F32),
                        pltpu.VMEM((ts + 2 * POOL_MAXW, POOL_W), F32),
                        pltpu.VMEM((ts + 2 * POOL_MAXW, POOL_W), F32),
                        pltpu.VMEM((ts + SUBLANES, CONV_W), F32)],
        compiler_params=pltpu.CompilerParams(dimension_semantics=("arbitrary", "arbitrary"),
                                             vmem_limit_bytes=VMEM_LIMIT),
        name="token_mixers",
    )(x, g1, *rw_in, watt, qg, kg, cos, sin, sink_rows, wpc, poolw, pscale, convw)


def _merge_kernel(x_ref, g1_ref, wg_ref, za_ref, zb_ref, zc_ref, zd_ref,
                  wa_ref, wb_ref, wc_ref, wd_ref, wo_ref, o_ref):
    x = x_ref[...]
    h = _bf(_rmsnorm(x, g1_ref[...]))
    mixed = None
    for b, (z_ref, w_ref) in enumerate(((za_ref, wa_ref), (zb_ref, wb_ref), (zc_ref, wc_ref), (zd_ref, wd_ref))):
        gate = jax.nn.sigmoid(_dot(h, wg_ref[:, D_MODEL * b:D_MODEL * (b + 1)]))
        term = gate * _dot(z_ref[...], w_ref[...])
        mixed = term if mixed is None else mixed + term
    o_ref[...] = x + _dot(_bf(mixed), wo_ref[...])


def _merge_call(x2, g1, wg, za, zb, zc, zd, wa, wb, wc, wd, wo, tm):
    t = x2.shape[0]
    const = lambda shape: pl.BlockSpec(shape, lambda i: (0,) * len(shape))
    tile = lambda w: pl.BlockSpec((tm, w), lambda i: (i, 0))
    wout = const((RW_W, D_MODEL))
    return pl.pallas_call(
        _merge_kernel,
        out_shape=jax.ShapeDtypeStruct((t, D_MODEL), F32),
        grid=(t // tm,),
        in_specs=[tile(D_MODEL), const((1, D_MODEL)), const((D_MODEL, N_BRANCH * D_MODEL)),
                  tile(RW_W), tile(POOL_W), tile(CONV_W), tile(ATT_HQ * ATT_HD),
                  wout, wout, wout, wout, const((D_MODEL, D_MODEL))],
        out_specs=tile(D_MODEL),
        compiler_params=pltpu.CompilerParams(dimension_semantics=("arbitrary",),
                                             vmem_limit_bytes=VMEM_LIMIT),
        name="merge_mixers",
    )(x2, g1, wg, za, zb, zc, zd, wa, wb, wc, wd, wo)


def _ffn_kernel(x_ref, g2_ref, wg_ref, wu_ref, wd_ref, o_ref):
    x = x_ref[...]
    h = _bf(_rmsnorm(x, g2_ref[...]))
    acc = x
    for c in range(D_FF // FFN_CHUNK):
        cs = slice(FFN_CHUNK * c, FFN_CHUNK * (c + 1))
        gt = _dot(h, wg_ref[:, cs])
        up = _dot(h, wu_ref[:, cs])
        act = _bf(gt * jax.nn.sigmoid(gt) * up)
        acc = acc + _dot(act, wd_ref[cs, :])
    o_ref[...] = acc


def _ffn_call(x2, g2, wg, wu, wd, tm):
    t = x2.shape[0]
    const = lambda shape: pl.BlockSpec(shape, lambda i: (0,) * len(shape))
    tile = pl.BlockSpec((tm, D_MODEL), lambda i: (i, 0))
    return pl.pallas_call(
        _ffn_kernel,
        out_shape=jax.ShapeDtypeStruct((t, D_MODEL), F32),
        grid=(t // tm,),
        in_specs=[tile, const((1, D_MODEL)), const((D_MODEL, D_FF)), const((D_MODEL, D_FF)),
                  const((D_FF, D_MODEL))],
        out_specs=tile,
        compiler_params=pltpu.CompilerParams(dimension_semantics=("arbitrary",),
                                             vmem_limit_bytes=VMEM_LIMIT),
        name="ffn_swiglu",
    )(x2, g2, wg, wu, wd)


def _attn_perms():
    half = ATT_HD // 2
    q_cols = []
    for jb in range(ATT_G):
        for hf in range(2):
            for ab in range(ATT_HKV):
                head = ATT_G * ab + jb
                q_cols += [ATT_HD * head + half * hf + i for i in range(half)]
    k_cols = []
    for hf in range(2):
        for g in range(ATT_HKV):
            k_cols += [ATT_HD * g + half * hf + i for i in range(half)]
    gain_idx = [half * ((l % LANES) // (LANES // 2)) + l % half for l in range(LANES)]
    o_rows = []
    for jb in range(ATT_G):
        for ab in range(ATT_HKV):
            head = ATT_G * ab + jb
            o_rows += [ATT_HD * head + c for c in range(ATT_HD)]
    return q_cols, k_cols, gain_idx, o_rows


def kernel(x, positions, norm1_g, w_in, shift_mu, w_decay_up, w0, a_up, a0, g_up, k_k, k_a, r_k, lnx_g, lnx_b, w_rwkv_out, pool_w, pool_scale, w_pool_out, conv_w, w_conv_out, q_norm_g, k_norm_g, sinks, w_attn_out, w_o, norm2_g, w_ffn_gate, w_ffn_up, w_ffn_down):
    b, s, d = x.shape
    assert d == D_MODEL and s % ATT_BLOCK == 0
    ts = min(SEQ_TILE, s)
    tm = min(TOK_TILE, b * s)
    assert s % ts == 0 and (b * s) % tm == 0
    depth = w_in.shape[0]

    q_cols, k_cols, gain_idx, o_rows = _attn_perms()
    q_cols = jnp.asarray(q_cols, jnp.int32)
    k_cols = jnp.asarray(k_cols, jnp.int32)
    gain_idx = jnp.asarray(gain_idx, jnp.int32)
    o_rows = jnp.asarray(o_rows, jnp.int32)
    c_rw, c_pool, c_conv, c_att = RW_COLS, RW_COLS + POOL_W, RW_COLS + POOL_W + 3 * CONV_W, \
        RW_COLS + POOL_W + 3 * CONV_W + ATT_COLS

    cos, sin = _rope_tables(positions, ts)
    row = lambda v: v.reshape(1, -1).astype(F32)
    zeros_lora = jnp.zeros((DECAY_LORA, RW_W), F32)

    for i in range(depth):
        g1 = row(norm1_g[i])
        wi = w_in[i]
        wdp = _bf(jnp.concatenate([w_decay_up[i], zeros_lora], axis=0))
        wap = _bf(jnp.concatenate([zeros_lora, a_up[i]], axis=0))
        rw_in = (_bf(wi[:, :c_rw]), row(shift_mu[i]), wdp, wap, _bf(g_up[i]),
                 row(w0[i]), row(a0[i]), row(k_k[i]), row(k_a[i]), row(r_k[i]),
                 row(lnx_g[i]), row(lnx_b[i]))
        w_att = wi[:, c_conv:c_att]
        w_att = jnp.concatenate([w_att[:, q_cols], w_att[:, ATT_HQ * ATT_HD + k_cols],
                                 w_att[:, (ATT_HQ + ATT_HKV) * ATT_HD:]], axis=1)
        sink_rows = jnp.repeat(sinks[i].astype(F32).reshape(ATT_HKV, ATT_G), ATT_BLOCK, axis=1)[..., None]
        za, zd, zb, zc = _mixers_call(
            x, g1, rw_in, _bf(w_att), row(q_norm_g[i][gain_idx]), row(k_norm_g[i][gain_idx]),
            cos, sin, sink_rows, _bf(wi[:, c_rw:c_conv]), _bf(pool_w[i]),
            row(pool_scale[i]), conv_w[i].astype(F32), ts)
        flat = lambda z: z.reshape(b * s, z.shape[-1])
        x1 = _merge_call(flat(x), g1, _bf(wi[:, c_att:]), flat(za), flat(zb), flat(zc), flat(zd),
                         _bf(w_rwkv_out[i]), _bf(w_pool_out[i]), _bf(w_conv_out[i]),
                         _bf(w_attn_out[i][o_rows, :]), _bf(w_o[i]), tm)
        x2 = _ffn_call(x1, row(norm2_g[i]), _bf(w_ffn_gate[i]), _bf(w_ffn_up[i]), _bf(w_ffn_down[i]), tm)
        x = x2.reshape(b, s, d)
    return x
```

```python
import math

import jax
import jax.numpy as jnp
from jax import lax
from jax.experimental import pallas as pl
from jax.experimental.pallas import tpu as pltpu

F32 = jnp.float32
BF16 = jnp.bfloat16

D_MODEL = 1024
RW_HEADS = 8
HEAD_DIM = 64
RW_W = RW_HEADS * HEAD_DIM
DECAY_LORA = 64
ICLR_LORA = 64
GATE_LORA = 128
LNX_EPS = 64e-5
RW_COLS = 3 * RW_W + DECAY_LORA + ICLR_LORA + GATE_LORA
POOL_W = 512
POOL_GW = 128
POOL_WINDOWS = (2, 4, 8, 16)
POOL_MAXW = 16
CONV_W = 512
CONV_K = 3
ATT_HQ = 8
ATT_HKV = 2
ATT_G = ATT_HQ // ATT_HKV
ATT_HD = 64
ATT_BLOCK = 128
ATT_COLS = (ATT_HQ + 2 * ATT_HKV) * ATT_HD
ROPE_THETA = 10000.0
N_BRANCH = 4
D_FF = 2816
NORM_EPS = 1e-6

LANES = 128
SUBLANES = 8
WKV_CHUNK = 64
WKV_CHUNKS_PER_STEP = 4
ATT_BLOCKS_PER_STEP = 2
ATT_STAGE_DELAY = 6
SEQ_TILE = 512
TOK_TILE = 1024
FFN_CHUNK = 256
VMEM_LIMIT = 48 * 1024 * 1024


def _bf(x):
    return x.astype(BF16)


def _dot(a, b):
    return jnp.dot(a, b, preferred_element_type=F32)


def _dot_nt(a, b):
    return lax.dot_general(a, b, (((1,), (1,)), ((), ())), preferred_element_type=F32)


def _split2(x):
    hi = _bf(x)
    lo = _bf(x - hi.astype(F32))
    return hi, lo


def _rmsnorm(x, g):
    ms = jnp.mean(x * x, axis=-1, keepdims=True)
    return x * lax.rsqrt(ms + NORM_EPS) * g


def _iota(shape, dim):
    return lax.broadcasted_iota(jnp.int32, shape, dim)


def _rope_kernel(pos_ref, inv_ref, cos_ref, sin_ref):
    ang = pos_ref[0].astype(F32) * inv_ref[...]
    lane = _iota(ang.shape, 1)
    cos_ref[0] = jnp.cos(ang)
    s = jnp.sin(ang)
    sin_ref[0] = jnp.where(lane < LANES // 2, -s, s)


def _rope_tables(positions, ts):
    b, s = positions.shape
    half = ATT_HD // 2
    inv = ROPE_THETA ** (-jnp.arange(half, dtype=F32) * 2.0 / ATT_HD)
    inv = jnp.tile(inv, LANES // half)[None, :]
    pos3 = positions.reshape(b, s, 1)
    out = jax.ShapeDtypeStruct((b, s, LANES), F32)
    return pl.pallas_call(
        _rope_kernel,
        out_shape=(out, out),
        grid=(b, s // ts),
        in_specs=[pl.BlockSpec((1, ts, 1), lambda i, j: (i, j, 0)),
                  pl.BlockSpec((1, LANES), lambda i, j: (0, 0))],
        out_specs=(pl.BlockSpec((1, ts, LANES), lambda i, j: (i, j, 0)),
                   pl.BlockSpec((1, ts, LANES), lambda i, j: (i, j, 0))),
        compiler_params=pltpu.CompilerParams(dimension_semantics=("arbitrary", "arbitrary")),
        name="rope_tables",
    )(pos3, inv)


def _rwkv_init(pbuf, st_s):
    st_s[...] = jnp.zeros(st_s.shape, F32)
    pbuf[0:SUBLANES, :] = jnp.zeros((SUBLANES, RW_COLS), F32)


def _rwkv_body(h, wrw_ref, mu_ref, wdp_ref, wap_ref, gup_ref, w0_ref, a0_ref,
               kk_ref, ka_ref, rk_ref, lng_ref, lnb_ref, o_ref,
               pbuf, r_s, k_s, v_s, kap_s, beta_s, lw_s, bonus_s, g_s, y_s, st_s, background):
    tb = h.shape[0]
    L = WKV_CHUNK

    gi = _iota((LANES, LANES), 0) // HEAD_DIM
    gj = _iota((LANES, LANES), 1) // HEAD_DIM
    seg_ones = jnp.where(gi == gj, 1.0, 0.0).astype(BF16)

    def segsum(z):
        return jnp.concatenate(
            [_dot(_bf(z[:, LANES * b:LANES * (b + 1)]), seg_ones) for b in range(RW_W // LANES)], axis=1)

    p = _dot(h, wrw_ref[...])
    for g_ in background.get('first', ()):
        next(g_, 'done')
    pbuf[SUBLANES:SUBLANES + tb, :] = p
    p_prev = pbuf[SUBLANES - 1:SUBLANES - 1 + tb, :]
    pbuf[SUBLANES - 1:SUBLANES, :] = p[tb - 1:tb, :]
    pm = p + (p_prev - p) * mu_ref[...]
    r = pm[:, 0:RW_W]
    k = pm[:, RW_W:2 * RW_W]
    v = pm[:, 2 * RW_W:3 * RW_W]
    lora_in = pm[:, 3 * RW_W:3 * RW_W + LANES]
    gd = pm[:, 3 * RW_W + LANES:RW_COLS]
    z = w0_ref[...] + _dot(_bf(jnp.tanh(lora_in)), wdp_ref[...])
    lw_s[...] = (-math.exp(-0.5)) * jax.nn.sigmoid(z)
    a = jax.nn.sigmoid(a0_ref[...] + _dot(_bf(lora_in), wap_ref[...]))
    g_s[...] = _dot(_bf(jax.nn.sigmoid(gd)), gup_ref[...])
    kk = k * kk_ref[...]
    kap = kk * lax.rsqrt(jnp.maximum(segsum(kk * kk), 1e-24))
    k2 = k * (1.0 + (a - 1.0) * ka_ref[...])
    r_s[...] = r
    k_s[...] = k2
    v_s[...] = v
    kap_s[...] = kap
    beta_s[...] = kap * a
    bonus_s[...] = segsum(r * k2 * rk_ref[...]) * v

    assert L == HEAD_DIM
    lane_lo = _iota((L, LANES), 1) < HEAD_DIM
    tok = _iota((L, LANES), 0)
    col = _iota((L, LANES), 1) % L
    strict = col < tok
    incl = col <= tok
    eye_sbs = jnp.where(col == tok, 1.0, 0.0)
    same_head = (_iota((LANES, LANES), 0) // HEAD_DIM) == (_iota((LANES, LANES), 1) // HEAD_DIM)
    ltri2 = jnp.where(_iota((L, 2 * L), 1) % L <= _iota((L, 2 * L), 0), 1.0, 0.0).astype(BF16)

    def stack(zz):
        zero = jnp.zeros_like(zz)
        return jnp.concatenate([jnp.where(lane_lo, zz, zero), jnp.where(lane_lo, zero, zz)], axis=0)

    n_pairs = RW_W // LANES
    n_ch = min(WKV_CHUNKS_PER_STEP, tb // L)

    def phase1(gidx, chains):
        for q in range(n_ch):
            t0 = (gidx * n_ch + q) * L
            rows = slice(t0, t0 + L)
            lwc = lw_s[rows, :]
            c = _dot(ltri2, jnp.concatenate(_split2(lwc), axis=0))
            c_last = c[L - 1:L, :]
            e_in = jnp.exp(c)
            e_prev = jnp.exp(c - lwc)
            e_out = jnp.exp(-c)
            g_end = jnp.exp(c_last)
            e_end = g_end * e_out
            kc = k_s[rows, :]
            vc = _bf(v_s[rows, :])
            betac = beta_s[rows, :]
            rt = _bf(r_s[rows, :] * e_in)
            kt = _bf(kc * e_out)
            bt = _bf(betac * e_out)
            kapt = _bf(kap_s[rows, :] * e_prev)
            khat = kc * e_end
            bhat = betac * e_end
            for pr in range(n_pairs):
                sl = slice(LANES * pr, LANES * (pr + 1))
                kts = stack(kt[:, sl])
                bts = stack(bt[:, sl])
                chains.append(dict(
                    q=q, pr=pr, rows=rows, sl=sl, rt=rt[:, sl], kapt=kapt[:, sl], v=vc[:, sl],
                    kb=jnp.concatenate([bts, kts], axis=0), kapts=stack(kapt[:, sl]),
                    vs=stack(vc[:, sl]), khat=khat[:, sl], bhat=bhat[:, sl], g_end=g_end[:, sl]))
            yield
        for ch in chains:
            ma = _dot_nt(jnp.concatenate([ch['kapt'], ch['rt']], axis=0), ch['kb'])
            ch['m_ab'] = jnp.where(strict, ma[:L, :LANES], 0.0)
            ch['m_ak'] = _bf(jnp.where(strict, ma[:L, LANES:], 0.0))
            ch['a_qb'] = _bf(jnp.where(incl, ma[L:, :LANES], 0.0))
            ch['a_qk'] = _bf(jnp.where(incl, ma[L:, LANES:], 0.0))
        yield
        for ch in chains:
            ch['t'] = eye_sbs - ch['m_ab']
            ch['pw'] = _bf(ch['m_ab'])
        for ch in chains:
            ch['pw'] = _bf(_dot(ch['pw'], stack(ch['pw'])))
        yield
        for it in range(5):
            for ch in chains:
                ch['pws'] = stack(ch['pw'])
            if it < 4:
                for ch in chains:
                    both = _dot(jnp.concatenate([ch['pw'], _bf(ch['t'])], axis=0), ch['pws'])
                    ch['pw_next'] = both[:L]
                    ch['t'] = ch['t'] + both[L:]
            else:
                for ch in chains:
                    ch['t'] = ch['t'] + _dot(_bf(ch['t']), ch['pws'])
            if it == 0:
                for ch in chains:
                    mvy = _dot(jnp.concatenate([ch['m_ak'], ch['a_qk']], axis=0), ch['vs'])
                    ch['mvs'] = stack(_bf(mvy[:L]))
                    ch['y0'] = mvy[L:]
            if it == 1:
                for ch in chains:
                    ch['kb_t'] = jnp.concatenate([_bf(ch['khat'].T), _bf(ch['bhat'].T)], axis=1)
                    ch['g_rows'] = jnp.broadcast_to(ch['g_end'], (LANES, LANES)).T
            if it < 4:
                for ch in chains:
                    ch['pw'] = _bf(ch['pw_next'])
            yield
        for ch in chains:
            wu = _dot(_bf(ch['t']), jnp.concatenate([ch['kapts'], ch['mvs']], axis=1))
            ch['u0'] = wu[:, LANES:]
            ch['wr'] = jnp.concatenate([_bf(wu[:, :LANES]), ch['rt']], axis=0)
        yield

    def phase2(chains, st):
        for q in range(n_ch):
            cq = [ch for ch in chains if ch['q'] == q]
            st_b = [_bf(s_) for s_ in st]
            ws = [_dot(ch['wr'], st_b[ch['pr']]) for ch in cq]
            u_b = [_bf(ws[ch['pr']][:L] + ch['u0']) for ch in cq]
            yield
            st[:] = [ch['g_rows'] * st[ch['pr']]
                     + jnp.where(same_head, _dot(ch['kb_t'], jnp.concatenate([ch['v'], -u_b[ch['pr']]], axis=0)), 0.0)
                     for ch in cq]
            for ch in cq:
                y_s[ch['rows'], ch['sl']] = (ws[ch['pr']][L:] + ch['y0']
                                             - _dot(ch['a_qb'], stack(u_b[ch['pr']])))
            yield

    def stage_c(gidx):
        rows = slice(gidx * n_ch * L, (gidx + 1) * n_ch * L)
        y = y_s[rows, :]
        mean = segsum(y) * (1.0 / HEAD_DIM)
        yield
        d = y - mean
        var = segsum(d * d) * (1.0 / HEAD_DIM)
        yield
        yn = d * lax.rsqrt(var + LNX_EPS) * lng_ref[...] + lnb_ref[...]
        o_ref[0, rows, :] = _bf((yn + bonus_s[rows, :]) * g_s[rows, :])
        yield

    st = [st_s[pr] for pr in range(n_pairs)]
    chains_of = {}
    n_groups = tb // (L * n_ch)
    for step in range(n_groups + 2):
        gens = []
        if step < n_groups:
            chains_of[step] = []
            gens.append(phase1(step, chains_of[step]))
        if 0 <= step - 1 < n_groups:
            gens.append(phase2(chains_of.pop(step - 1), st))
        if 0 <= step - 2 < n_groups:
            gens.append(stage_c(step - 2))
        _interleave(gens, background.get(step, ()))
    for pr in range(n_pairs):
        st_s[pr] = st[pr]
    _interleave([g_ for gs_ in background.values() for g_ in gs_])


def _poolconv_init(ubuf, sbuf_a, sbuf_b, cbuf):
    for buf in (ubuf, sbuf_a, sbuf_b):
        buf[0:2 * POOL_MAXW, :] = jnp.zeros((2 * POOL_MAXW, POOL_W), F32)
    cbuf[0:SUBLANES, :] = jnp.zeros((SUBLANES, CONV_W), F32)


def _poolconv_stages(j, h, wpc_ref, poolw_ref, pscale_ref, convw_ref, ob_ref, oc_ref,
                     ubuf, sbuf_a, sbuf_b, cbuf):
    tb = h.shape[0]
    pad = POOL_MAXW
    lo = 2 * POOL_MAXW
    u = _dot(h, wpc_ref[:, 0:POOL_W])
    ubuf[lo:lo + tb, :] = u
    yield
    assert POOL_WINDOWS == tuple(2 ** (gi + 1) for gi in range(len(POOL_WINDOWS)))
    src = ubuf
    sums = []
    for gi, win in enumerate(POOL_WINDOWS):
        dst = sbuf_a if gi % 2 == 0 else sbuf_b
        cols = slice(POOL_GW * gi, POOL_W)
        shift = win // 2
        dst[pad:lo + tb, cols] = src[pad:lo + tb, cols] + src[pad - shift:lo + tb - shift, cols]
        sums.append(dst)
        src = dst
        yield
    t_glob = j * tb + _iota((tb, 1), 0)
    for gi, win in enumerate(POOL_WINDOWS):
        cs = slice(POOL_GW * gi, POOL_GW * (gi + 1))
        cnt = jnp.minimum(t_glob + 1, win).astype(F32)
        zc = sums[gi][lo:lo + tb, cs] / cnt - u[:, cs]
        zz = _dot(_bf(zc), poolw_ref[gi])
        ob_ref[0, :, cs] = _bf(zz * pscale_ref[:, cs])
        yield
    ubuf[pad:lo, :] = ubuf[tb + pad:tb + lo, :]

    cg = _dot(h, wpc_ref[:, POOL_W + CONV_W:POOL_W + 2 * CONV_W])
    yield
    cu = _dot(h, wpc_ref[:, POOL_W + 2 * CONV_W:POOL_W + 3 * CONV_W])
    vv = cg * cu
    cbuf[SUBLANES:SUBLANES + tb, :] = vv
    yield
    bg = _dot(h, wpc_ref[:, POOL_W:POOL_W + CONV_W])
    conv = (convw_ref[0:1, :] * cbuf[SUBLANES - 2:SUBLANES - 2 + tb, :]
            + convw_ref[1:2, :] * cbuf[SUBLANES - 1:SUBLANES - 1 + tb, :]
            + convw_ref[2:3, :] * vv)
    cbuf[0:SUBLANES, :] = cbuf[tb:tb + SUBLANES, :]
    oc_ref[0] = _bf(bg * conv)
    yield


def _attn_init(sink_ref, kbuf0, kbuf1, vbuf, bias_tab):
    blk = ATT_BLOCK
    rows = ATT_G * blk
    kbuf0[0:blk, :] = jnp.zeros((blk, LANES), BF16)
    kbuf1[0:blk, :] = jnp.zeros((blk, LANES), BF16)
    vbuf[0:blk, :] = jnp.zeros((blk, LANES), BF16)
    qi = _iota((rows, 2 * blk), 0) % blk + blk
    kj = _iota((rows, 2 * blk), 1)
    dist = qi - kj
    band = (dist >= 0) & (dist < ATT_BLOCK)
    for g in range(ATT_HKV):
        sink = sink_ref[g]
        bias_tab[0, g] = jnp.where(kj == 0, sink, jnp.where(band & (kj >= blk), 0.0, -jnp.inf))
        bias_tab[1, g] = jnp.where(kj == 0, sink, jnp.where(band, 0.0, -jnp.inf))


def _attn_stages(j, h, watt_ref, qg_ref, kg_ref, cos_ref, sin_ref, o_ref, kbuf0, kbuf1, vbuf, bias_tab):
    tq = h.shape[0]
    blk = ATT_BLOCK
    nq = ATT_HQ * ATT_HD // LANES
    rows = nq * blk

    hi_ = (_iota((LANES, LANES), 0) % ATT_HD) // (ATT_HD // 2)
    hj_ = (_iota((LANES, LANES), 1) % ATT_HD) // (ATT_HD // 2)
    seg_mean = jnp.where(hi_ == hj_, 1.0 / ATT_HD, 0.0).astype(BF16)

    p = _dot(h, watt_ref[...])
    yield
    cos = cos_ref[0]
    sin = sin_ref[0]

    def norm_rope(xb, gain):
        ms = _dot(_bf(xb * xb), seg_mean)
        yb = xb * lax.rsqrt(ms + NORM_EPS) * gain
        return yb * cos + pltpu.roll(yb, LANES // 2, axis=1) * sin

    kn = norm_rope(p[:, nq * LANES:(nq + 1) * LANES], kg_ref[...])
    kv_lane = (_iota((tq, LANES), 1) % ATT_HD) // (ATT_HD // 2)
    kbuf0[blk:blk + tq, :] = _bf(jnp.where(kv_lane == 0, kn, 0.0))
    kbuf1[blk:blk + tq, :] = _bf(jnp.where(kv_lane == 1, kn, 0.0))
    vbuf[blk:blk + tq, :] = _bf(p[:, (nq + 1) * LANES:(nq + 2) * LANES])
    yield
    qs = [_bf(norm_rope(p[:, LANES * b:LANES * (b + 1)], qg_ref[...]) * (ATT_HD ** -0.5))
          for b in range(nq)]
    yield

    out_lo = _iota((rows, LANES), 1) < ATT_HD
    not_sink_row = _iota((2 * blk, LANES), 0) > 0
    ones_cols = jnp.ones((2 * blk, LANES), BF16)
    zero_kv = jnp.zeros((2 * blk, LANES), BF16)
    first_tab = jnp.where(j == 0, 0, 1)
    nblk = tq // blk
    nstep = min(ATT_BLOCKS_PER_STEP, nblk)
    for n0 in range(0, nblk, nstep):
        chains = []
        for n in range(n0, n0 + nstep):
            q_st = jnp.concatenate([q[blk * n:blk * (n + 1), :] for q in qs], axis=0)
            win = slice(blk * n, blk * (n + 2))
            v1 = jnp.concatenate([jnp.where(not_sink_row, vbuf[win, :], zero_kv), ones_cols], axis=1)
            for g, kbuf in enumerate((kbuf0, kbuf1)):
                chains.append(dict(n=n, g=g, q=q_st, v1=v1,
                                   kb=jnp.where(not_sink_row, kbuf[win, :], zero_kv)))
        for ch in chains:
            bias = bias_tab[first_tab, ch['g']] if ch['n'] == 0 else bias_tab[1, ch['g']]
            ch['sc'] = _dot_nt(ch['q'], ch['kb']) + bias
        yield
        for ch in chains:
            ch['m'] = jnp.max(ch['sc'], axis=-1, keepdims=True)
        yield
        for ch in chains:
            ch['e'] = _bf(jnp.exp(ch['sc'] - ch['m']))
        yield
        for ch in chains:
            ch['o'] = _dot(ch['e'], ch['v1'])
        yield
        for n in range(n0, n0 + nstep):
            c0, c1 = [ch for ch in chains if ch['n'] == n]
            num = jnp.where(out_lo, c0['o'][:, :LANES], c1['o'][:, :LANES])
            den = jnp.where(out_lo, c0['o'][:, LANES:], c1['o'][:, LANES:])
            o = num / den
            for b in range(nq):
                o_ref[0, blk * n:blk * (n + 1), LANES * b:LANES * (b + 1)] = _bf(o[blk * b:blk * (b + 1), :])
    for buf in (kbuf0, kbuf1, vbuf):
        buf[0:blk, :] = buf[tq:tq + blk, :]


def _interleave(gens, background=()):
    while gens:
        gens = [g_ for g_ in gens if next(g_, 'done') != 'done']
        for g_ in background:
            next(g_, 'done')


N_RWKV_IN = 12
N_RWKV_SCRATCH = 11


def _mixers_kernel(x_ref, g1_ref, *refs):
    rw_in, refs = refs[:N_RWKV_IN], refs[N_RWKV_IN:]
    (watt_ref, qg_ref, kg_ref, cos_ref, sin_ref, sink_ref, wpc_ref, poolw_ref, pscale_ref,
     convw_ref, oa_ref, od_ref, ob_ref, oc_ref) = refs[:14]
    rw_scr = refs[14:14 + N_RWKV_SCRATCH]
    kbuf0, kbuf1, vbuf, bias_tab, ubuf, sbuf_a, sbuf_b, cbuf = refs[14 + N_RWKV_SCRATCH:]
    j = pl.program_id(1)

    @pl.when(j == 0)
    def _():
        _rwkv_init(rw_scr[0], rw_scr[-1])
        _attn_init(sink_ref, kbuf0, kbuf1, vbuf, bias_tab)
        _poolconv_init(ubuf, sbuf_a, sbuf_b, cbuf)

    h = _bf(_rmsnorm(x_ref[0], g1_ref[...]))
    attn = _attn_stages(j, h, watt_ref, qg_ref, kg_ref, cos_ref, sin_ref, od_ref, kbuf0, kbuf1, vbuf, bias_tab)
    poolconv = _poolconv_stages(j, h, wpc_ref, poolw_ref, pscale_ref, convw_ref, ob_ref, oc_ref,
                                ubuf, sbuf_a, sbuf_b, cbuf)

    def delayed(gen, rounds):
        for _ in range(rounds):
            yield
        yield from gen

    attn_late = delayed(attn, ATT_STAGE_DELAY)
    _rwkv_body(h, *rw_in, oa_ref, *rw_scr,
               background={'first': [attn, poolconv], 1: [attn_late], 2: [attn_late, poolconv]})


def _mixers_call(x, g1, rw_in, watt, qg, kg, cos, sin, sink_rows, wpc, poolw, pscale, convw, ts):
    b, s, _ = x.shape
    const = lambda shape: pl.BlockSpec(shape, lambda i, j: (0,) * len(shape))
    tile = lambda w: pl.BlockSpec((1, ts, w), lambda i, j: (i, j, 0))
    out = lambda w: jax.ShapeDtypeStruct((b, s, w), BF16)
    row = const((1, RW_W))
    scr = lambda: pltpu.VMEM((ts, RW_W), F32)
    assert len(rw_in) == N_RWKV_IN
    return pl.pallas_call(
        _mixers_kernel,
        out_shape=(out(RW_W), out(ATT_HQ * ATT_HD), out(POOL_W), out(CONV_W)),
        grid=(b, s // ts),
        in_specs=[tile(D_MODEL), const((1, D_MODEL)),
                  const((D_MODEL, RW_COLS)), const((1, RW_COLS)),
                  const((LANES, RW_W)), const((LANES, RW_W)), const((GATE_LORA, RW_W)),
                  row, row, row, row, row, row, row,
                  const((D_MODEL, ATT_COLS)),
                  const((1, LANES)), const((1, LANES)), tile(LANES), tile(LANES),
                  const((ATT_HKV, ATT_G * ATT_BLOCK, 1)),
                  const((D_MODEL, POOL_W + 3 * CONV_W)),
                  const((len(POOL_WINDOWS), POOL_GW, POOL_GW)), const((1, POOL_W)),
                  const((CONV_K, CONV_W))],
        out_specs=(tile(RW_W), tile(ATT_HQ * ATT_HD), tile(POOL_W), tile(CONV_W)),
        scratch_shapes=[pltpu.VMEM((ts + SUBLANES, RW_COLS), F32),
                        scr(), scr(), scr(), scr(), scr(), scr(), scr(), scr(), scr(),
                        pltpu.VMEM((RW_W // LANES, LANES, LANES), F32),
                        pltpu.VMEM((ts + ATT_BLOCK, LANES), BF16),
                        pltpu.VMEM((ts + ATT_BLOCK, LANES), BF16),
                        pltpu.VMEM((ts + ATT_BLOCK, LANES), BF16),
                        pltpu.VMEM((2, ATT_HKV, ATT_G * ATT_BLOCK, 2 * ATT_BLOCK), F32),
                        pltpu.VMEM((ts + 2 * POOL_MAXW, POOL_W), F32),
                        pltpu.VMEM((ts + 2 * POOL_MAXW, POOL_W), F32),
                        pltpu.VMEM((ts + 2 * POOL_MAXW, POOL_W), F32),
                        pltpu.VMEM((ts + SUBLANES, CONV_W), F32)],
        compiler_params=pltpu.CompilerParams(dimension_semantics=("arbitrary", "arbitrary"),
                                             vmem_limit_bytes=VMEM_LIMIT),
        name="token_mixers",
    )(x, g1, *rw_in, watt, qg, kg, cos, sin, sink_rows, wpc, poolw, pscale, convw)


def _merge_kernel(x_ref, g1_ref, wg_ref, za_ref, zb_ref, zc_ref, zd_ref,
                  wa_ref, wb_ref, wc_ref, wd_ref, wo_ref, o_ref):
    x = x_ref[...]
    h = _bf(_rmsnorm(x, g1_ref[...]))
    mixed = None
    for b, (z_ref, w_ref) in enumerate(((za_ref, wa_ref), (zb_ref, wb_ref), (zc_ref, wc_ref), (zd_ref, wd_ref))):
        gate = jax.nn.sigmoid(_dot(h, wg_ref[:, D_MODEL * b:D_MODEL * (b + 1)]))
        term = gate * _dot(z_ref[...], w_ref[...])
        mixed = term if mixed is None else mixed + term
    o_ref[...] = x + _dot(_bf(mixed), wo_ref[...])


def _merge_call(x2, g1, wg, za, zb, zc, zd, wa, wb, wc, wd, wo, tm):
    t = x2.shape[0]
    const = lambda shape: pl.BlockSpec(shape, lambda i: (0,) * len(shape))
    tile = lambda w: pl.BlockSpec((tm, w), lambda i: (i, 0))
    wout = const((RW_W, D_MODEL))
    return pl.pallas_call(
        _merge_kernel,
        out_shape=jax.ShapeDtypeStruct((t, D_MODEL), F32),
        grid=(t // tm,),
        in_specs=[tile(D_MODEL), const((1, D_MODEL)), const((D_MODEL, N_BRANCH * D_MODEL)),
                  tile(RW_W), tile(POOL_W), tile(CONV_W), tile(ATT_HQ * ATT_HD),
                  wout, wout, wout, wout, const((D_MODEL, D_MODEL))],
        out_specs=tile(D_MODEL),
        compiler_params=pltpu.CompilerParams(dimension_semantics=("arbitrary",),
                                             vmem_limit_bytes=VMEM_LIMIT),
        name="merge_mixers",
    )(x2, g1, wg, za, zb, zc, zd, wa, wb, wc, wd, wo)


def _ffn_kernel(x_ref, g2_ref, wg_ref, wu_ref, wd_ref, o_ref):
    x = x_ref[...]
    h = _bf(_rmsnorm(x, g2_ref[...]))
    acc = x
    for c in range(D_FF // FFN_CHUNK):
        cs = slice(FFN_CHUNK * c, FFN_CHUNK * (c + 1))
        gt = _dot(h, wg_ref[:, cs])
        up = _dot(h, wu_ref[:, cs])
        act = _bf(gt * jax.nn.sigmoid(gt) * up)
        acc = acc + _dot(act, wd_ref[cs, :])
    o_ref[...] = acc


def _ffn_call(x2, g2, wg, wu, wd, tm):
    t = x2.shape[0]
    const = lambda shape: pl.BlockSpec(shape, lambda i: (0,) * len(shape))
    tile = pl.BlockSpec((tm, D_MODEL), lambda i: (i, 0))
    return pl.pallas_call(
        _ffn_kernel,
        out_shape=jax.ShapeDtypeStruct((t, D_MODEL), F32),
        grid=(t // tm,),
        in_specs=[tile, const((1, D_MODEL)), const((D_MODEL, D_FF)), const((D_MODEL, D_FF)),
                  const((D_FF, D_MODEL))],
        out_specs=tile,
        compiler_params=pltpu.CompilerParams(dimension_semantics=("arbitrary",),
                                             vmem_limit_bytes=VMEM_LIMIT),
        name="ffn_swiglu",
    )(x2, g2, wg, wu, wd)


def _attn_perms():
    half = ATT_HD // 2
    q_cols = []
    for jb in range(ATT_G):
        for hf in range(2):
            for ab in range(ATT_HKV):
                head = ATT_G * ab + jb
                q_cols += [ATT_HD * head + half * hf + i for i in range(half)]
    k_cols = []
    for hf in range(2):
        for g in range(ATT_HKV):
            k_cols += [ATT_HD * g + half * hf + i for i in range(half)]
    gain_idx = [half * ((l % LANES) // (LANES // 2)) + l % half for l in range(LANES)]
    o_rows = []
    for jb in range(ATT_G):
        for ab in range(ATT_HKV):
            head = ATT_G * ab + jb
            o_rows += [ATT_HD * head + c for c in range(ATT_HD)]
    return q_cols, k_cols, gain_idx, o_rows


def kernel(x, positions, norm1_g, w_in, shift_mu, w_decay_up, w0, a_up, a0, g_up, k_k, k_a, r_k, lnx_g, lnx_b, w_rwkv_out, pool_w, pool_scale, w_pool_out, conv_w, w_conv_out, q_norm_g, k_norm_g, sinks, w_attn_out, w_o, norm2_g, w_ffn_gate, w_ffn_up, w_ffn_down):
    b, s, d = x.shape
    assert d == D_MODEL and s % ATT_BLOCK == 0
    ts = min(SEQ_TILE, s)
    tm = min(TOK_TILE, b * s)
    assert s % ts == 0 and (b * s) % tm == 0
    depth = w_in.shape[0]

    q_cols, k_cols, gain_idx, o_rows = _attn_perms()
    q_cols = jnp.asarray(q_cols, jnp.int32)
    k_cols = jnp.asarray(k_cols, jnp.int32)
    gain_idx = jnp.asarray(gain_idx, jnp.int32)
    o_rows = jnp.asarray(o_rows, jnp.int32)
    c_rw, c_pool, c_conv, c_att = RW_COLS, RW_COLS + POOL_W, RW_COLS + POOL_W + 3 * CONV_W, \
        RW_COLS + POOL_W + 3 * CONV_W + ATT_COLS

    cos, sin = _rope_tables(positions, ts)
    row = lambda v: v.reshape(1, -1).astype(F32)
    zeros_lora = jnp.zeros((DECAY_LORA, RW_W), F32)

    for i in range(depth):
        g1 = row(norm1_g[i])
        wi = w_in[i]
        wdp = _bf(jnp.concatenate([w_decay_up[i], zeros_lora], axis=0))
        wap = _bf(jnp.concatenate([zeros_lora, a_up[i]], axis=0))
        rw_in = (_bf(wi[:, :c_rw]), row(shift_mu[i]), wdp, wap, _bf(g_up[i]),
                 row(w0[i]), row(a0[i]), row(k_k[i]), row(k_a[i]), row(r_k[i]),
                 row(lnx_g[i]), row(lnx_b[i]))
        w_att = wi[:, c_conv:c_att]
        w_att = jnp.concatenate([w_att[:, q_cols], w_att[:, ATT_HQ * ATT_HD + k_cols],
                                 w_att[:, (ATT_HQ + ATT_HKV) * ATT_HD:]], axis=1)
        sink_rows = jnp.repeat(sinks[i].astype(F32).reshape(ATT_HKV, ATT_G), ATT_BLOCK, axis=1)[..., None]
        za, zd, zb, zc = _mixers_call(
            x, g1, rw_in, _bf(w_att), row(q_norm_g[i][gain_idx]), row(k_norm_g[i][gain_idx]),
            cos, sin, sink_rows, _bf(wi[:, c_rw:c_conv]), _bf(pool_w[i]),
            row(pool_scale[i]), conv_w[i].astype(F32), ts)
        flat = lambda z: z.reshape(b * s, z.shape[-1])
        x1 = _merge_call(flat(x), g1, _bf(wi[:, c_att:]), flat(za), flat(zb), flat(zc), flat(zd),
                         _bf(w_rwkv_out[i]), _bf(w_pool_out[i]), _bf(w_conv_out[i]),
                         _bf(w_attn_out[i][o_rows, :]), _bf(w_o[i]), tm)
        x2 = _ffn_call(x1, row(norm2_g[i]), _bf(w_ffn_gate[i]), _bf(w_ffn_up[i]), _bf(w_ffn_down[i]), tm)
        x = x2.reshape(b, s, d)
    return x
```

```python
import math

import jax
import jax.numpy as jnp
from jax import lax
from jax.experimental import pallas as pl
from jax.experimental.pallas import tpu as pltpu

F32 = jnp.float32
BF16 = jnp.bfloat16

D_MODEL = 1024
RW_HEADS = 8
HEAD_DIM = 64
RW_W = RW_HEADS * HEAD_DIM
DECAY_LORA = 64
ICLR_LORA = 64
GATE_LORA = 128
LNX_EPS = 64e-5
RW_COLS = 3 * RW_W + DECAY_LORA + ICLR_LORA + GATE_LORA
POOL_W = 512
POOL_GW = 128
POOL_WINDOWS = (2, 4, 8, 16)
POOL_MAXW = 16
CONV_W = 512
CONV_K = 3
ATT_HQ = 8
ATT_HKV = 2
ATT_G = ATT_HQ // ATT_HKV
ATT_HD = 64
ATT_BLOCK = 128
ATT_COLS = (ATT_HQ + 2 * ATT_HKV) * ATT_HD
ROPE_THETA = 10000.0
N_BRANCH = 4
D_FF = 2816
NORM_EPS = 1e-6

LANES = 128
SUBLANES = 8
WKV_CHUNK = 64
WKV_CHUNKS_PER_STEP = 4
ATT_BLOCKS_PER_STEP = 2
ATT_STAGE_DELAY = 6
SEQ_TILE = 512
TOK_TILE = 1024
FFN_CHUNK = 256
VMEM_LIMIT = 48 * 1024 * 1024
CHANNEL_VMEM_LIMIT = 56 * 1024 * 1024
CHANNEL_TILE = 512


def _bf(x):
    return x.astype(BF16)


def _dot(a, b):
    return jnp.dot(a, b, preferred_element_type=F32)


def _dot_nt(a, b):
    return lax.dot_general(a, b, (((1,), (1,)), ((), ())), preferred_element_type=F32)


def _split2(x):
    hi = _bf(x)
    lo = _bf(x - hi.astype(F32))
    return hi, lo


def _rmsnorm(x, g):
    ms = jnp.mean(x * x, axis=-1, keepdims=True)
    return x * lax.rsqrt(ms + NORM_EPS) * g


def _iota(shape, dim):
    return lax.broadcasted_iota(jnp.int32, shape, dim)


def _rope_kernel(pos_ref, inv_ref, cos_ref, sin_ref):
    ang = pos_ref[0].astype(F32) * inv_ref[...]
    lane = _iota(ang.shape, 1)
    cos_ref[0] = jnp.cos(ang)
    s = jnp.sin(ang)
    sin_ref[0] = jnp.where(lane < LANES // 2, -s, s)


def _rope_tables(positions, ts):
    b, s = positions.shape
    half = ATT_HD // 2
    inv = ROPE_THETA ** (-jnp.arange(half, dtype=F32) * 2.0 / ATT_HD)
    inv = jnp.tile(inv, LANES // half)[None, :]
    pos3 = positions.reshape(b, s, 1)
    out = jax.ShapeDtypeStruct((b, s, LANES), F32)
    return pl.pallas_call(
        _rope_kernel,
        out_shape=(out, out),
        grid=(b, s // ts),
        in_specs=[pl.BlockSpec((1, ts, 1), lambda i, j: (i, j, 0)),
                  pl.BlockSpec((1, LANES), lambda i, j: (0, 0))],
        out_specs=(pl.BlockSpec((1, ts, LANES), lambda i, j: (i, j, 0)),
                   pl.BlockSpec((1, ts, LANES), lambda i, j: (i, j, 0))),
        compiler_params=pltpu.CompilerParams(dimension_semantics=("arbitrary", "arbitrary")),
        name="rope_tables",
    )(pos3, inv)


def _rwkv_init(pbuf, st_s):
    st_s[...] = jnp.zeros(st_s.shape, F32)
    pbuf[0:SUBLANES, :] = jnp.zeros((SUBLANES, RW_COLS), F32)


def _rwkv_body(h, wrw_ref, mu_ref, wdp_ref, wap_ref, gup_ref, w0_ref, a0_ref,
               kk_ref, ka_ref, rk_ref, lng_ref, lnb_ref, o_ref,
               pbuf, r_s, k_s, v_s, kap_s, beta_s, lw_s, bonus_s, g_s, y_s, st_s, background):
    tb = h.shape[0]
    L = WKV_CHUNK

    gi = _iota((LANES, LANES), 0) // HEAD_DIM
    gj = _iota((LANES, LANES), 1) // HEAD_DIM
    seg_ones = jnp.where(gi == gj, 1.0, 0.0).astype(BF16)

    def segsum(z):
        return jnp.concatenate(
            [_dot(_bf(z[:, LANES * b:LANES * (b + 1)]), seg_ones) for b in range(RW_W // LANES)], axis=1)

    p = _dot(h, wrw_ref[...])
    for g_ in background.get('first', ()):
        next(g_, 'done')
    pbuf[SUBLANES:SUBLANES + tb, :] = p
    p_prev = pbuf[SUBLANES - 1:SUBLANES - 1 + tb, :]
    pbuf[SUBLANES - 1:SUBLANES, :] = p[tb - 1:tb, :]
    pm = p + (p_prev - p) * mu_ref[...]
    r = pm[:, 0:RW_W]
    k = pm[:, RW_W:2 * RW_W]
    v = pm[:, 2 * RW_W:3 * RW_W]
    lora_in = pm[:, 3 * RW_W:3 * RW_W + LANES]
    gd = pm[:, 3 * RW_W + LANES:RW_COLS]
    z = w0_ref[...] + _dot(_bf(jnp.tanh(lora_in)), wdp_ref[...])
    lw_s[...] = (-math.exp(-0.5)) * jax.nn.sigmoid(z)
    a = jax.nn.sigmoid(a0_ref[...] + _dot(_bf(lora_in), wap_ref[...]))
    g_s[...] = _dot(_bf(jax.nn.sigmoid(gd)), gup_ref[...])
    kk = k * kk_ref[...]
    kap = kk * lax.rsqrt(jnp.maximum(segsum(kk * kk), 1e-24))
    k2 = k * (1.0 + (a - 1.0) * ka_ref[...])
    r_s[...] = r
    k_s[...] = k2
    v_s[...] = v
    kap_s[...] = kap
    beta_s[...] = kap * a
    bonus_s[...] = segsum(r * k2 * rk_ref[...]) * v

    assert L == HEAD_DIM
    lane_lo = _iota((L, LANES), 1) < HEAD_DIM
    tok = _iota((L, LANES), 0)
    col = _iota((L, LANES), 1) % L
    strict = col < tok
    incl = col <= tok
    eye_sbs = jnp.where(col == tok, 1.0, 0.0)
    same_head = (_iota((LANES, LANES), 0) // HEAD_DIM) == (_iota((LANES, LANES), 1) // HEAD_DIM)
    ltri2 = jnp.where(_iota((L, 2 * L), 1) % L <= _iota((L, 2 * L), 0), 1.0, 0.0).astype(BF16)

    def stack(zz):
        zero = jnp.zeros_like(zz)
        return jnp.concatenate([jnp.where(lane_lo, zz, zero), jnp.where(lane_lo, zero, zz)], axis=0)

    n_pairs = RW_W // LANES
    n_ch = min(WKV_CHUNKS_PER_STEP, tb // L)

    def phase1(gidx, chains):
        for q in range(n_ch):
            t0 = (gidx * n_ch + q) * L
            rows = slice(t0, t0 + L)
            lwc = lw_s[rows, :]
            c = _dot(ltri2, jnp.concatenate(_split2(lwc), axis=0))
            c_last = c[L - 1:L, :]
            e_in = jnp.exp(c)
            e_prev = jnp.exp(c - lwc)
            e_out = jnp.exp(-c)
            g_end = jnp.exp(c_last)
            e_end = g_end * e_out
            kc = k_s[rows, :]
            vc = _bf(v_s[rows, :])
            betac = beta_s[rows, :]
            rt = _bf(r_s[rows, :] * e_in)
            kt = _bf(kc * e_out)
            bt = _bf(betac * e_out)
            kapt = _bf(kap_s[rows, :] * e_prev)
            khat = kc * e_end
            bhat = betac * e_end
            for pr in range(n_pairs):
                sl = slice(LANES * pr, LANES * (pr + 1))
                kts = stack(kt[:, sl])
                bts = stack(bt[:, sl])
                chains.append(dict(
                    q=q, pr=pr, rows=rows, sl=sl, rt=rt[:, sl], kapt=kapt[:, sl], v=vc[:, sl],
                    kb=jnp.concatenate([bts, kts], axis=0), kapts=stack(kapt[:, sl]),
                    vs=stack(vc[:, sl]), khat=khat[:, sl], bhat=bhat[:, sl], g_end=g_end[:, sl]))
            yield
        for ch in chains:
            ma = _dot_nt(jnp.concatenate([ch['kapt'], ch['rt']], axis=0), ch['kb'])
            ch['m_ab'] = jnp.where(strict, ma[:L, :LANES], 0.0)
            ch['m_ak'] = _bf(jnp.where(strict, ma[:L, LANES:], 0.0))
            ch['a_qb'] = _bf(jnp.where(incl, ma[L:, :LANES], 0.0))
            ch['a_qk'] = _bf(jnp.where(incl, ma[L:, LANES:], 0.0))
        yield
        for ch in chains:
            ch['t'] = eye_sbs - ch['m_ab']
            ch['pw'] = _bf(ch['m_ab'])
        for ch in chains:
            ch['pw'] = _bf(_dot(ch['pw'], stack(ch['pw'])))
        yield
        for it in range(5):
            for ch in chains:
                ch['pws'] = stack(ch['pw'])
            if it < 4:
                for ch in chains:
                    both = _dot(jnp.concatenate([ch['pw'], _bf(ch['t'])], axis=0), ch['pws'])
                    ch['pw_next'] = both[:L]
                    ch['t'] = ch['t'] + both[L:]
            else:
                for ch in chains:
                    ch['t'] = ch['t'] + _dot(_bf(ch['t']), ch['pws'])
            if it == 0:
                for ch in chains:
                    mvy = _dot(jnp.concatenate([ch['m_ak'], ch['a_qk']], axis=0), ch['vs'])
                    ch['mvs'] = stack(_bf(mvy[:L]))
                    ch['y0'] = mvy[L:]
            if it == 1:
                for ch in chains:
                    ch['kb_t'] = jnp.concatenate([_bf(ch['khat'].T), _bf(ch['bhat'].T)], axis=1)
                    ch['g_rows'] = jnp.broadcast_to(ch['g_end'], (LANES, LANES)).T
            if it < 4:
                for ch in chains:
                    ch['pw'] = _bf(ch['pw_next'])
            yield
        for ch in chains:
            wu = _dot(_bf(ch['t']), jnp.concatenate([ch['kapts'], ch['mvs']], axis=1))
            ch['u0'] = wu[:, LANES:]
            ch['wr'] = jnp.concatenate([_bf(wu[:, :LANES]), ch['rt']], axis=0)
        yield

    def phase2(chains, st):
        for q in range(n_ch):
            cq = [ch for ch in chains if ch['q'] == q]
            st_b = [_bf(s_) for s_ in st]
            ws = [_dot(ch['wr'], st_b[ch['pr']]) for ch in cq]
            u_b = [_bf(ws[ch['pr']][:L] + ch['u0']) for ch in cq]
            yield
            st[:] = [ch['g_rows'] * st[ch['pr']]
                     + jnp.where(same_head, _dot(ch['kb_t'], jnp.concatenate([ch['v'], -u_b[ch['pr']]], axis=0)), 0.0)
                     for ch in cq]
            for ch in cq:
                y_s[ch['rows'], ch['sl']] = (ws[ch['pr']][L:] + ch['y0']
                                             - _dot(ch['a_qb'], stack(u_b[ch['pr']])))
            yield

    def stage_c(gidx):
        rows = slice(gidx * n_ch * L, (gidx + 1) * n_ch * L)
        y = y_s[rows, :]
        mean = segsum(y) * (1.0 / HEAD_DIM)
        yield
        d = y - mean
        var = segsum(d * d) * (1.0 / HEAD_DIM)
        yield
        yn = d * lax.rsqrt(var + LNX_EPS) * lng_ref[...] + lnb_ref[...]
        o_ref[0, rows, :] = _bf((yn + bonus_s[rows, :]) * g_s[rows, :])
        yield

    st = [st_s[pr] for pr in range(n_pairs)]
    chains_of = {}
    n_groups = tb // (L * n_ch)
    for step in range(n_groups + 2):
        gens = []
        if step < n_groups:
            chains_of[step] = []
            gens.append(phase1(step, chains_of[step]))
        if 0 <= step - 1 < n_groups:
            gens.append(phase2(chains_of.pop(step - 1), st))
        if 0 <= step - 2 < n_groups:
            gens.append(stage_c(step - 2))
        _interleave(gens, background.get(step, ()))
    for pr in range(n_pairs):
        st_s[pr] = st[pr]
    _interleave([g_ for gs_ in background.values() for g_ in gs_])


def _poolconv_init(ubuf, sbuf_a, sbuf_b, cbuf):
    for buf in (ubuf, sbuf_a, sbuf_b):
        buf[0:2 * POOL_MAXW, :] = jnp.zeros((2 * POOL_MAXW, POOL_W), F32)
    cbuf[0:SUBLANES, :] = jnp.zeros((SUBLANES, CONV_W), F32)


def _poolconv_stages(j, h, wpc_ref, poolw_ref, pscale_ref, convw_ref, ob_ref, oc_ref,
                     ubuf, sbuf_a, sbuf_b, cbuf):
    tb = h.shape[0]
    pad = POOL_MAXW
    lo = 2 * POOL_MAXW
    u = _dot(h, wpc_ref[:, 0:POOL_W])
    ubuf[lo:lo + tb, :] = u
    yield
    assert POOL_WINDOWS == tuple(2 ** (gi + 1) for gi in range(len(POOL_WINDOWS)))
    src = ubuf
    sums = []
    for gi, win in enumerate(POOL_WINDOWS):
        dst = sbuf_a if gi % 2 == 0 else sbuf_b
        cols = slice(POOL_GW * gi, POOL_W)
        shift = win // 2
        dst[pad:lo + tb, cols] = src[pad:lo + tb, cols] + src[pad - shift:lo + tb - shift, cols]
        sums.append(dst)
        src = dst
        yield
    t_glob = j * tb + _iota((tb, 1), 0)
    for gi, win in enumerate(POOL_WINDOWS):
        cs = slice(POOL_GW * gi, POOL_GW * (gi + 1))
        cnt = jnp.minimum(t_glob + 1, win).astype(F32)
        zc = sums[gi][lo:lo + tb, cs] / cnt - u[:, cs]
        zz = _dot(_bf(zc), poolw_ref[gi])
        ob_ref[0, :, cs] = _bf(zz * pscale_ref[:, cs])
        yield
    ubuf[pad:lo, :] = ubuf[tb + pad:tb + lo, :]

    cg = _dot(h, wpc_ref[:, POOL_W + CONV_W:POOL_W + 2 * CONV_W])
    yield
    cu = _dot(h, wpc_ref[:, POOL_W + 2 * CONV_W:POOL_W + 3 * CONV_W])
    vv = cg * cu
    cbuf[SUBLANES:SUBLANES + tb, :] = vv
    yield
    bg = _dot(h, wpc_ref[:, POOL_W:POOL_W + CONV_W])
    conv = (convw_ref[0:1, :] * cbuf[SUBLANES - 2:SUBLANES - 2 + tb, :]
            + convw_ref[1:2, :] * cbuf[SUBLANES - 1:SUBLANES - 1 + tb, :]
            + convw_ref[2:3, :] * vv)
    cbuf[0:SUBLANES, :] = cbuf[tb:tb + SUBLANES, :]
    oc_ref[0] = _bf(bg * conv)
    yield


def _attn_init(sink_ref, kbuf0, kbuf1, vbuf, bias_tab):
    blk = ATT_BLOCK
    rows = ATT_G * blk
    kbuf0[0:blk, :] = jnp.zeros((blk, LANES), BF16)
    kbuf1[0:blk, :] = jnp.zeros((blk, LANES), BF16)
    vbuf[0:blk, :] = jnp.zeros((blk, LANES), BF16)
    qi = _iota((rows, 2 * blk), 0) % blk + blk
    kj = _iota((rows, 2 * blk), 1)
    dist = qi - kj
    band = (dist >= 0) & (dist < ATT_BLOCK)
    for g in range(ATT_HKV):
        sink = sink_ref[g]
        bias_tab[0, g] = jnp.where(kj == 0, sink, jnp.where(band & (kj >= blk), 0.0, -jnp.inf))
        bias_tab[1, g] = jnp.where(kj == 0, sink, jnp.where(band, 0.0, -jnp.inf))


def _attn_stages(j, h, watt_ref, qg_ref, kg_ref, cos_ref, sin_ref, o_ref, kbuf0, kbuf1, vbuf, bias_tab):
    tq = h.shape[0]
    blk = ATT_BLOCK
    nq = ATT_HQ * ATT_HD // LANES
    rows = nq * blk

    hi_ = (_iota((LANES, LANES), 0) % ATT_HD) // (ATT_HD // 2)
    hj_ = (_iota((LANES, LANES), 1) % ATT_HD) // (ATT_HD // 2)
    seg_mean = jnp.where(hi_ == hj_, 1.0 / ATT_HD, 0.0).astype(BF16)

    p = _dot(h, watt_ref[...])
    yield
    cos = cos_ref[0]
    sin = sin_ref[0]

    def norm_rope(xb, gain):
        ms = _dot(_bf(xb * xb), seg_mean)
        yb = xb * lax.rsqrt(ms + NORM_EPS) * gain
        return yb * cos + pltpu.roll(yb, LANES // 2, axis=1) * sin

    kn = norm_rope(p[:, nq * LANES:(nq + 1) * LANES], kg_ref[...])
    kv_lane = (_iota((tq, LANES), 1) % ATT_HD) // (ATT_HD // 2)
    kbuf0[blk:blk + tq, :] = _bf(jnp.where(kv_lane == 0, kn, 0.0))
    kbuf1[blk:blk + tq, :] = _bf(jnp.where(kv_lane == 1, kn, 0.0))
    vbuf[blk:blk + tq, :] = _bf(p[:, (nq + 1) * LANES:(nq + 2) * LANES])
    yield
    qs = [_bf(norm_rope(p[:, LANES * b:LANES * (b + 1)], qg_ref[...]) * (ATT_HD ** -0.5))
          for b in range(nq)]
    yield

    out_lo = _iota((rows, LANES), 1) < ATT_HD
    not_sink_row = _iota((2 * blk, LANES), 0) > 0
    ones_cols = jnp.ones((2 * blk, LANES), BF16)
    zero_kv = jnp.zeros((2 * blk, LANES), BF16)
    first_tab = jnp.where(j == 0, 0, 1)
    nblk = tq // blk
    nstep = min(ATT_BLOCKS_PER_STEP, nblk)
    for n0 in range(0, nblk, nstep):
        chains = []
        for n in range(n0, n0 + nstep):
            q_st = jnp.concatenate([q[blk * n:blk * (n + 1), :] for q in qs], axis=0)
            win = slice(blk * n, blk * (n + 2))
            v1 = jnp.concatenate([jnp.where(not_sink_row, vbuf[win, :], zero_kv), ones_cols], axis=1)
            for g, kbuf in enumerate((kbuf0, kbuf1)):
                chains.append(dict(n=n, g=g, q=q_st, v1=v1,
                                   kb=jnp.where(not_sink_row, kbuf[win, :], zero_kv)))
        for ch in chains:
            bias = bias_tab[first_tab, ch['g']] if ch['n'] == 0 else bias_tab[1, ch['g']]
            ch['sc'] = _dot_nt(ch['q'], ch['kb']) + bias
        yield
        for ch in chains:
            ch['m'] = jnp.max(ch['sc'], axis=-1, keepdims=True)
        yield
        for ch in chains:
            ch['e'] = _bf(jnp.exp(ch['sc'] - ch['m']))
        yield
        for ch in chains:
            ch['o'] = _dot(ch['e'], ch['v1'])
        yield
        for n in range(n0, n0 + nstep):
            c0, c1 = [ch for ch in chains if ch['n'] == n]
            num = jnp.where(out_lo, c0['o'][:, :LANES], c1['o'][:, :LANES])
            den = jnp.where(out_lo, c0['o'][:, LANES:], c1['o'][:, LANES:])
            o = num / den
            for b in range(nq):
                o_ref[0, blk * n:blk * (n + 1), LANES * b:LANES * (b + 1)] = _bf(o[blk * b:blk * (b + 1), :])
    for buf in (kbuf0, kbuf1, vbuf):
        buf[0:blk, :] = buf[tq:tq + blk, :]


def _interleave(gens, background=()):
    while gens:
        gens = [g_ for g_ in gens if next(g_, 'done') != 'done']
        for g_ in background:
            next(g_, 'done')


N_RWKV_IN = 12
N_RWKV_SCRATCH = 11


def _mixers_kernel(x_ref, g1_ref, *refs):
    rw_in, refs = refs[:N_RWKV_IN], refs[N_RWKV_IN:]
    (watt_ref, qg_ref, kg_ref, cos_ref, sin_ref, sink_ref, wpc_ref, poolw_ref, pscale_ref,
     convw_ref, oa_ref, od_ref, ob_ref, oc_ref) = refs[:14]
    rw_scr = refs[14:14 + N_RWKV_SCRATCH]
    kbuf0, kbuf1, vbuf, bias_tab, ubuf, sbuf_a, sbuf_b, cbuf = refs[14 + N_RWKV_SCRATCH:]
    j = pl.program_id(1)

    @pl.when(j == 0)
    def _():
        _rwkv_init(rw_scr[0], rw_scr[-1])
        _attn_init(sink_ref, kbuf0, kbuf1, vbuf, bias_tab)
        _poolconv_init(ubuf, sbuf_a, sbuf_b, cbuf)

    h = _bf(_rmsnorm(x_ref[0], g1_ref[...]))
    attn = _attn_stages(j, h, watt_ref, qg_ref, kg_ref, cos_ref, sin_ref, od_ref, kbuf0, kbuf1, vbuf, bias_tab)
    poolconv = _poolconv_stages(j, h, wpc_ref, poolw_ref, pscale_ref, convw_ref, ob_ref, oc_ref,
                                ubuf, sbuf_a, sbuf_b, cbuf)

    def delayed(gen, rounds):
        for _ in range(rounds):
            yield
        yield from gen

    attn_late = delayed(attn, ATT_STAGE_DELAY)
    _rwkv_body(h, *rw_in, oa_ref, *rw_scr,
               background={'first': [attn, poolconv], 1: [attn_late], 2: [attn_late, poolconv]})


def _mixers_call(x, g1, rw_in, watt, qg, kg, cos, sin, sink_rows, wpc, poolw, pscale, convw, ts):
    b, s, _ = x.shape
    const = lambda shape: pl.BlockSpec(shape, lambda i, j: (0,) * len(shape))
    tile = lambda w: pl.BlockSpec((1, ts, w), lambda i, j: (i, j, 0))
    out = lambda w: jax.ShapeDtypeStruct((b, s, w), BF16)
    row = const((1, RW_W))
    scr = lambda: pltpu.VMEM((ts, RW_W), F32)
    assert len(rw_in) == N_RWKV_IN
    return pl.pallas_call(
        _mixers_kernel,
        out_shape=(out(RW_W), out(ATT_HQ * ATT_HD), out(POOL_W), out(CONV_W)),
        grid=(b, s // ts),
        in_specs=[tile(D_MODEL), const((1, D_MODEL)),
                  const((D_MODEL, RW_COLS)), const((1, RW_COLS)),
                  const((LANES, RW_W)), const((LANES, RW_W)), const((GATE_LORA, RW_W)),
                  row, row, row, row, row, row, row,
                  const((D_MODEL, ATT_COLS)),
                  const((1, LANES)), const((1, LANES)), tile(LANES), tile(LANES),
                  const((ATT_HKV, ATT_G * ATT_BLOCK, 1)),
                  const((D_MODEL, POOL_W + 3 * CONV_W)),
                  const((len(POOL_WINDOWS), POOL_GW, POOL_GW)), const((1, POOL_W)),
                  const((CONV_K, CONV_W))],
        out_specs=(tile(RW_W), tile(ATT_HQ * ATT_HD), tile(POOL_W), tile(CONV_W)),
        scratch_shapes=[pltpu.VMEM((ts + SUBLANES, RW_COLS), F32),
                        scr(), scr(), scr(), scr(), scr(), scr(), scr(), scr(), scr(),
                        pltpu.VMEM((RW_W // LANES, LANES, LANES), F32),
                        pltpu.VMEM((ts + ATT_BLOCK, LANES), BF16),
                        pltpu.VMEM((ts + ATT_BLOCK, LANES), BF16),
                        pltpu.VMEM((ts + ATT_BLOCK, LANES), BF16),
                        pltpu.VMEM((2, ATT_HKV, ATT_G * ATT_BLOCK, 2 * ATT_BLOCK), F32),
                        pltpu.VMEM((ts + 2 * POOL_MAXW, POOL_W), F32),
                        pltpu.VMEM((ts + 2 * POOL_MAXW, POOL_W), F32),
                        pltpu.VMEM((ts + 2 * POOL_MAXW, POOL_W), F32),
                        pltpu.VMEM((ts + SUBLANES, CONV_W), F32)],
        compiler_params=pltpu.CompilerParams(dimension_semantics=("arbitrary", "arbitrary"),
                                             vmem_limit_bytes=VMEM_LIMIT),
        name="token_mixers",
    )(x, g1, *rw_in, watt, qg, kg, cos, sin, sink_rows, wpc, poolw, pscale, convw)


def _merge_kernel(x_ref, g1_ref, wg_ref, za_ref, zb_ref, zc_ref, zd_ref,
                  wa_ref, wb_ref, wc_ref, wd_ref, wo_ref, o_ref):
    x = x_ref[...]
    h = _bf(_rmsnorm(x, g1_ref[...]))
    mixed = None
    for b, (z_ref, w_ref) in enumerate(((za_ref, wa_ref), (zb_ref, wb_ref), (zc_ref, wc_ref), (zd_ref, wd_ref))):
        gate = jax.nn.sigmoid(_dot(h, wg_ref[:, D_MODEL * b:D_MODEL * (b + 1)]))
        term = gate * _dot(z_ref[...], w_ref[...])
        mixed = term if mixed is None else mixed + term
    o_ref[...] = x + _dot(_bf(mixed), wo_ref[...])


def _merge_call(x2, g1, wg, za, zb, zc, zd, wa, wb, wc, wd, wo, tm):
    t = x2.shape[0]
    const = lambda shape: pl.BlockSpec(shape, lambda i: (0,) * len(shape))
    tile = lambda w: pl.BlockSpec((tm, w), lambda i: (i, 0))
    wout = const((RW_W, D_MODEL))
    return pl.pallas_call(
        _merge_kernel,
        out_shape=jax.ShapeDtypeStruct((t, D_MODEL), F32),
        grid=(t // tm,),
        in_specs=[tile(D_MODEL), const((1, D_MODEL)), const((D_MODEL, N_BRANCH * D_MODEL)),
                  tile(RW_W), tile(POOL_W), tile(CONV_W), tile(ATT_HQ * ATT_HD),
                  wout, wout, wout, wout, const((D_MODEL, D_MODEL))],
        out_specs=tile(D_MODEL),
        compiler_params=pltpu.CompilerParams(dimension_semantics=("arbitrary",),
                                             vmem_limit_bytes=VMEM_LIMIT),
        name="merge_mixers",
    )(x2, g1, wg, za, zb, zc, zd, wa, wb, wc, wd, wo)


def _ffn_kernel(x_ref, g2_ref, wg_ref, wu_ref, wd_ref, o_ref):
    x = x_ref[...]
    h = _bf(_rmsnorm(x, g2_ref[...]))
    acc = x
    for c in range(D_FF // FFN_CHUNK):
        cs = slice(FFN_CHUNK * c, FFN_CHUNK * (c + 1))
        gt = _dot(h, wg_ref[:, cs])
        up = _dot(h, wu_ref[:, cs])
        act = _bf(gt * jax.nn.sigmoid(gt) * up)
        acc = acc + _dot(act, wd_ref[cs, :])
    o_ref[...] = acc


def _ffn_call(x2, g2, wg, wu, wd, tm):
    t = x2.shape[0]
    const = lambda shape: pl.BlockSpec(shape, lambda i: (0,) * len(shape))
    tile = pl.BlockSpec((tm, D_MODEL), lambda i: (i, 0))
    return pl.pallas_call(
        _ffn_kernel,
        out_shape=jax.ShapeDtypeStruct((t, D_MODEL), F32),
        grid=(t // tm,),
        in_specs=[tile, const((1, D_MODEL)), const((D_MODEL, D_FF)), const((D_MODEL, D_FF)),
                  const((D_FF, D_MODEL))],
        out_specs=tile,
        compiler_params=pltpu.CompilerParams(dimension_semantics=("arbitrary",),
                                             vmem_limit_bytes=VMEM_LIMIT),
        name="ffn_swiglu",
    )(x2, g2, wg, wu, wd)


def _merge_ffn_kernel(x_ref, g1_ref, wg_ref, za_ref, zb_ref, zc_ref, zd_ref,
                      wa_ref, wb_ref, wc_ref, wd_ref, wo_ref, g2_ref, fg_ref, fu_ref, fd_ref, o_ref):
    x = x_ref[...]
    h = _bf(_rmsnorm(x, g1_ref[...]))
    mixed = None
    for b, (z_ref, w_ref) in enumerate(((za_ref, wa_ref), (zb_ref, wb_ref), (zc_ref, wc_ref), (zd_ref, wd_ref))):
        gate = jax.nn.sigmoid(_dot(h, wg_ref[:, D_MODEL * b:D_MODEL * (b + 1)]))
        term = gate * _dot(z_ref[...], w_ref[...])
        mixed = term if mixed is None else mixed + term
    x1 = x + _dot(_bf(mixed), wo_ref[...])
    h2 = _bf(_rmsnorm(x1, g2_ref[...]))
    acc = x1
    for c in range(D_FF // FFN_CHUNK):
        cs = slice(FFN_CHUNK * c, FFN_CHUNK * (c + 1))
        gt = _dot(h2, fg_ref[:, cs])
        up = _dot(h2, fu_ref[:, cs])
        act = _bf(gt * jax.nn.sigmoid(gt) * up)
        acc = acc + _dot(act, fd_ref[cs, :])
    o_ref[...] = acc


def _merge_ffn_call(x2, g1, wg, za, zb, zc, zd, wa, wb, wc, wd, wo, g2, fg, fu, fd, tm):
    t = x2.shape[0]
    const = lambda shape: pl.BlockSpec(shape, lambda i: (0,) * len(shape))
    tile = lambda w: pl.BlockSpec((tm, w), lambda i: (i, 0))
    wout = const((RW_W, D_MODEL))
    return pl.pallas_call(
        _merge_ffn_kernel,
        out_shape=jax.ShapeDtypeStruct((t, D_MODEL), F32),
        grid=(t // tm,),
        in_specs=[tile(D_MODEL), const((1, D_MODEL)), const((D_MODEL, N_BRANCH * D_MODEL)),
                  tile(RW_W), tile(POOL_W), tile(CONV_W), tile(ATT_HQ * ATT_HD),
                  wout, wout, wout, wout, const((D_MODEL, D_MODEL)),
                  const((1, D_MODEL)), const((D_MODEL, D_FF)), const((D_MODEL, D_FF)), const((D_FF, D_MODEL))],
        out_specs=tile(D_MODEL),
        compiler_params=pltpu.CompilerParams(dimension_semantics=("arbitrary",),
                                             vmem_limit_bytes=CHANNEL_VMEM_LIMIT),
        name="merge_ffn",
    )(x2, g1, wg, za, zb, zc, zd, wa, wb, wc, wd, wo, g2, fg, fu, fd)


def _attn_perms():
    half = ATT_HD // 2
    q_cols = []
    for jb in range(ATT_G):
        for hf in range(2):
            for ab in range(ATT_HKV):
                head = ATT_G * ab + jb
                q_cols += [ATT_HD * head + half * hf + i for i in range(half)]
    k_cols = []
    for hf in range(2):
        for g in range(ATT_HKV):
            k_cols += [ATT_HD * g + half * hf + i for i in range(half)]
    gain_idx = [half * ((l % LANES) // (LANES // 2)) + l % half for l in range(LANES)]
    o_rows = []
    for jb in range(ATT_G):
        for ab in range(ATT_HKV):
            head = ATT_G * ab + jb
            o_rows += [ATT_HD * head + c for c in range(ATT_HD)]
    return q_cols, k_cols, gain_idx, o_rows


def kernel(x, positions, norm1_g, w_in, shift_mu, w_decay_up, w0, a_up, a0, g_up, k_k, k_a, r_k, lnx_g, lnx_b, w_rwkv_out, pool_w, pool_scale, w_pool_out, conv_w, w_conv_out, q_norm_g, k_norm_g, sinks, w_attn_out, w_o, norm2_g, w_ffn_gate, w_ffn_up, w_ffn_down):
    b, s, d = x.shape
    assert d == D_MODEL and s % ATT_BLOCK == 0
    ts = min(SEQ_TILE, s)
    tm = min(TOK_TILE, b * s)
    assert s % ts == 0 and (b * s) % tm == 0
    depth = w_in.shape[0]

    q_cols, k_cols, gain_idx, o_rows = _attn_perms()
    q_cols = jnp.asarray(q_cols, jnp.int32)
    k_cols = jnp.asarray(k_cols, jnp.int32)
    gain_idx = jnp.asarray(gain_idx, jnp.int32)
    o_rows = jnp.asarray(o_rows, jnp.int32)
    c_rw, c_pool, c_conv, c_att = RW_COLS, RW_COLS + POOL_W, RW_COLS + POOL_W + 3 * CONV_W, \
        RW_COLS + POOL_W + 3 * CONV_W + ATT_COLS

    cos, sin = _rope_tables(positions, ts)
    row = lambda v: v.reshape(1, -1).astype(F32)
    zeros_lora = jnp.zeros((DECAY_LORA, RW_W), F32)

    for i in range(depth):
        g1 = row(norm1_g[i])
        wi = w_in[i]
        wdp = _bf(jnp.concatenate([w_decay_up[i], zeros_lora], axis=0))
        wap = _bf(jnp.concatenate([zeros_lora, a_up[i]], axis=0))
        rw_in = (_bf(wi[:, :c_rw]), row(shift_mu[i]), wdp, wap, _bf(g_up[i]),
                 row(w0[i]), row(a0[i]), row(k_k[i]), row(k_a[i]), row(r_k[i]),
                 row(lnx_g[i]), row(lnx_b[i]))
        w_att = wi[:, c_conv:c_att]
        w_att = jnp.concatenate([w_att[:, q_cols], w_att[:, ATT_HQ * ATT_HD + k_cols],
                                 w_att[:, (ATT_HQ + ATT_HKV) * ATT_HD:]], axis=1)
        sink_rows = jnp.repeat(sinks[i].astype(F32).reshape(ATT_HKV, ATT_G), ATT_BLOCK, axis=1)[..., None]
        za, zd, zb, zc = _mixers_call(
            x, g1, rw_in, _bf(w_att), row(q_norm_g[i][gain_idx]), row(k_norm_g[i][gain_idx]),
            cos, sin, sink_rows, _bf(wi[:, c_rw:c_conv]), _bf(pool_w[i]),
            row(pool_scale[i]), conv_w[i].astype(F32), ts)
        flat = lambda z: z.reshape(b * s, z.shape[-1])
        x2 = _merge_ffn_call(flat(x), g1, _bf(wi[:, c_att:]), flat(za), flat(zb), flat(zc), flat(zd),
                             _bf(w_rwkv_out[i]), _bf(w_pool_out[i]), _bf(w_conv_out[i]),
                             _bf(w_attn_out[i][o_rows, :]), _bf(w_o[i]),
                             row(norm2_g[i]), _bf(w_ffn_gate[i]), _bf(w_ffn_up[i]), _bf(w_ffn_down[i]),
                             min(CHANNEL_TILE, b * s))
        x = x2.reshape(b, s, d)
    return x
```
